```python
import math
import jax, jax.numpy as jnp
from jax import lax
import numpy as np

D_MODEL = 1024
BATCH = 8
SEQ = 4096
DEPTH = 1
DEC_BATCH = 2
DEC_SEQ = 16384
PAST_LEN = 128

ATTN_WIDTH = D_MODEL // 2
CONV_WIDTH = D_MODEL - ATTN_WIDTH
HEAD_DIM = 64
N_ATTN_HEADS = ATTN_WIDTH // HEAD_DIM
ATTN_CONFIGS = ((128, 1), (512, 4), (2048, 16))
BLK = 64
CONV_K = 3
D_FF = ((int(math.ceil(8 * D_MODEL / 3)) + 255) // 256) * 256
IN_WIDTH = 3 * ATTN_WIDTH + 3 * CONV_WIDTH
ALPHA = (2.0 * DEPTH) ** 0.25
BETA = (8.0 * DEPTH) ** -0.25
LN_EPS = 1e-5
NEG = -1e30

kernel_name = "hymba_dilated_attn_shortconv_deepnorm_encoder"


def _alibi_slopes():
    return jnp.asarray(np.array([2.0 ** (-8.0 * (h + 1) / N_ATTN_HEADS) for h in range(N_ATTN_HEADS)], dtype=np.float32))


def _layernorm(x, g, b):
    xf = x.astype(jnp.float32)
    mu = jnp.mean(xf, axis=-1, keepdims=True)
    var = jnp.mean(jnp.square(xf - mu), axis=-1, keepdims=True)
    y = (xf - mu) * lax.rsqrt(var + LN_EPS) * g.astype(jnp.float32) + b.astype(jnp.float32)
    return y.astype(x.dtype)


def _rmsnorm(x, g):
    xf = x.astype(jnp.float32)
    y = xf * lax.rsqrt(jnp.mean(jnp.square(xf), axis=-1, keepdims=True) + LN_EPS) * g.astype(jnp.float32)
    return y.astype(x.dtype)


def _dilated_band_attention(q, k, v, window, dilation, slopes):
    B, S, H, Dh = q.shape
    d = dilation
    half = window // (2 * d)
    n = S // d
    nb = -(-n // BLK)
    n_pad = nb * BLK

    def to_sub(t):
        return t.reshape(B, n, d, H, Dh).transpose(0, 2, 1, 3, 4)

    qs = jnp.pad(to_sub(q), ((0, 0), (0, 0), (0, n_pad - n), (0, 0), (0, 0)))
    qs = qs.reshape(B, d, nb, BLK, H, Dh)
    pad_kv = ((0, 0), (0, 0), (BLK, n_pad - n + BLK), (0, 0), (0, 0))
    kp = jnp.pad(to_sub(k), pad_kv)
    vp = jnp.pad(to_sub(v), pad_kv)
    kb = jnp.concatenate([kp[:, :, o:o + n_pad].reshape(B, d, nb, BLK, H, Dh) for o in (0, BLK, 2 * BLK)], axis=3)
    vb = jnp.concatenate([vp[:, :, o:o + n_pad].reshape(B, d, nb, BLK, H, Dh) for o in (0, BLK, 2 * BLK)], axis=3)

    s = jnp.einsum('bdnqhc,bdnkhc->bdnhqk', qs, kb, preferred_element_type=jnp.float32) * (1.0 / math.sqrt(Dh))
    qi = jnp.arange(nb)[:, None] * BLK + jnp.arange(BLK)[None, :]
    kj = jnp.arange(nb)[:, None] * BLK - BLK + jnp.arange(3 * BLK)[None, :]
    rel = jnp.abs(qi[:, :, None] - kj[:, None, :])
    valid = ((kj >= 0) & (kj < n))[:, None, :] & (rel <= half)
    dist = (rel * d).astype(jnp.float32)
    bias = -slopes[None, :, None, None] * dist[:, None]
    s = jnp.where(valid[:, None], s + bias, NEG)
    m = jnp.max(s, axis=-1, keepdims=True)
    p = jnp.exp(s - m)
    den = jnp.sum(p, axis=-1, keepdims=True)
    lse = (m + jnp.log(den))[..., 0]
    o = jnp.einsum('bdnhqk,bdnkhc->bdnqhc', p / den, vb.astype(jnp.float32))
    o = o.reshape(B, d, n_pad, H, Dh)[:, :, :n].transpose(0, 2, 1, 3, 4).reshape(B, S, H, Dh)
    lse = lse.transpose(0, 1, 2, 4, 3).reshape(B, d, n_pad, H)[:, :, :n].transpose(0, 2, 1, 3).reshape(B, S, H)
    return o, lse


def _mixed_dilated_attention(q, k, v):
    slopes = _alibi_slopes()
    outs, lses = [], []
    for window, dilation in ATTN_CONFIGS:
        o, l = _dilated_band_attention(q, k, v, window, dilation, slopes)
        outs.append(o)
        lses.append(l)
    w = jax.nn.softmax(jnp.stack(lses, axis=-1), axis=-1)
    o = sum(outs[i] * w[..., i:i + 1] for i in range(len(ATTN_CONFIGS)))
    return o.astype(q.dtype)


def _short_conv(u, gate_b, gate_c, conv_w):
    h = gate_c * u
    hp = jnp.pad(h, ((0, 0), (1, 1), (0, 0)))
    c = conv_w[0] * hp[:, :-2] + conv_w[1] * hp[:, 1:-1] + conv_w[2] * hp[:, 2:]
    return gate_b * c


def _layer(x, w_in, conv_w, g_attn, g_conv, w_o, ln1_g, ln1_b, w_gate, w_up, w_down, ln2_g, ln2_b):
    B, S, _ = x.shape
    p = x @ w_in
    A = ATTN_WIDTH
    C = CONV_WIDTH
    q = p[..., 0:A].reshape(B, S, N_ATTN_HEADS, HEAD_DIM)
    k = p[..., A:2 * A].reshape(B, S, N_ATTN_HEADS, HEAD_DIM)
    v = p[..., 2 * A:3 * A].reshape(B, S, N_ATTN_HEADS, HEAD_DIM)
    u = p[..., 3 * A:3 * A + C]
    gate_b = p[..., 3 * A + C:3 * A + 2 * C]
    gate_c = p[..., 3 * A + 2 * C:3 * A + 3 * C]
    attn = _mixed_dilated_attention(q, k, v).reshape(B, S, A)
    conv = _short_conv(u, gate_b, gate_c, conv_w)
    mix = jnp.concatenate([_rmsnorm(attn, g_attn), _rmsnorm(conv, g_conv)], axis=-1) @ w_o
    x = _layernorm(ALPHA * x + mix, ln1_g, ln1_b)
    ffn = (jax.nn.silu(x @ w_gate) * (x @ w_up)) @ w_down
    x = _layernorm(ALPHA * x + ffn, ln2_g, ln2_b)
    return x


def setup_inputs(seed: int = 0) -> dict:
    key = jax.random.key(seed)
    ks = jax.random.split(key, 16)
    f32 = jnp.float32
    nrm = lambda k, s: jax.random.normal(k, s, dtype=f32)
    return {
        "x_prompt": nrm(ks[0], (BATCH, SEQ, D_MODEL)),
        "x_sample": nrm(ks[1], (DEC_BATCH, DEC_SEQ, D_MODEL)),
        "w_in": nrm(ks[2], (D_MODEL, IN_WIDTH)) * D_MODEL ** -0.5,
        "conv_w": nrm(ks[3], (CONV_K, CONV_WIDTH)) * CONV_K ** -0.5,
        "g_attn": 1.0 + 0.02 * nrm(ks[4], (ATTN_WIDTH,)),
        "g_conv": 1.0 + 0.02 * nrm(ks[5], (CONV_WIDTH,)),
        "w_o": nrm(ks[6], (D_MODEL, D_MODEL)) * (D_MODEL ** -0.5) * BETA,
        "ln1_g": 1.0 + 0.02 * nrm(ks[7], (D_MODEL,)),
        "ln1_b": 0.02 * nrm(ks[8], (D_MODEL,)),
        "w_gate": nrm(ks[9], (D_MODEL, D_FF)) * D_MODEL ** -0.5,
        "w_up": nrm(ks[10], (D_MODEL, D_FF)) * D_MODEL ** -0.5,
        "w_down": nrm(ks[11], (D_FF, D_MODEL)) * (D_FF ** -0.5) * BETA,
        "ln2_g": 1.0 + 0.02 * nrm(ks[12], (D_MODEL,)),
        "ln2_b": 0.02 * nrm(ks[13], (D_MODEL,)),
    }


def reference(x_prompt, x_sample, w_in, conv_w, g_attn, g_conv, w_o, ln1_g, ln1_b, w_gate, w_up, w_down, ln2_g, ln2_b):
    y_prompt = x_prompt
    y_sample = x_sample
    for _ in range(DEPTH):
        y_prompt = _layer(y_prompt, w_in, conv_w, g_attn, g_conv, w_o, ln1_g, ln1_b, w_gate, w_up, w_down, ln2_g, ln2_b)
        y_sample = _layer(y_sample, w_in, conv_w, g_attn, g_conv, w_o, ln1_g, ln1_b, w_gate, w_up, w_down, ln2_g, ln2_b)
    return (y_prompt, y_sample)
```

```python
import functools
import math

import numpy as np
import jax
import jax.numpy as jnp
from jax import lax
from jax.experimental import pallas as pl
from jax.experimental.pallas import tpu as pltpu

D_MODEL = 1024
ATTN_WIDTH = 512
CONV_WIDTH = 512
HEAD_DIM = 64
N_HEADS = 8
ATTN_CONFIGS = ((128, 1), (512, 4), (2048, 16))
BAND_HALF = 64
D_FF = 2816
DEPTH = 1
ALPHA = (2.0 * DEPTH) ** 0.25
LN_EPS = 1e-5
NEG = -1e30

LANES = 128
Q_BLOCK = 128
K_BLOCK = Q_BLOCK + 2 * BAND_HALF
HALO_ROWS_BF16 = 16
FFN_CHUNKS = ((0, 1024), (1024, 1024), (2048, 768))
VMEM_LIMIT = 56 * 1024 * 1024

F32 = jnp.float32
BF16 = jnp.bfloat16


def _inproj_kernel(x_ref, w_ref, q_ref, k_ref, v_ref, h_ref, gb_ref):
    xb = x_ref[...].astype(BF16)

    def proj(c):
        return jnp.dot(xb, w_ref[:, c * 512:(c + 1) * 512], preferred_element_type=F32)

    q_ref[...] = (proj(0) * (1.0 / math.sqrt(HEAD_DIM))).astype(BF16)
    k_ref[...] = proj(1).astype(BF16)
    v_ref[...] = proj(2).astype(BF16)
    u = proj(3)
    gb_ref[...] = proj(4).astype(BF16)
    h_ref[...] = (proj(5) * u).astype(BF16)


def _inproj(x2d, w_in_bf16, tile):
    nt = x2d.shape[0]
    out = jax.ShapeDtypeStruct((nt, 512), BF16)
    row_spec = pl.BlockSpec((tile, 512), lambda i: (i, 0))
    return pl.pallas_call(
        _inproj_kernel,
        grid=(nt // tile,),
        in_specs=[pl.BlockSpec((tile, D_MODEL), lambda i: (i, 0)),
                  pl.BlockSpec((D_MODEL, 6 * 512), lambda i: (0, 0))],
        out_specs=[row_spec] * 5,
        out_shape=[out] * 5,
        compiler_params=pltpu.CompilerParams(dimension_semantics=("arbitrary",),
                                             vmem_limit_bytes=VMEM_LIMIT),
        name="inproj",
    )(x2d, w_in_bf16)


def _bias_tables(dilation):
    slopes = np.array([2.0 ** (-8.0 * (h + 1) / N_HEADS) for h in range(N_HEADS)], np.float32)
    i = np.arange(Q_BLOCK)[:, None]
    j = np.arange(K_BLOCK)[None, :]
    rel = np.abs(i + BAND_HALF - j)
    band = rel <= BAND_HALF
    dist = (rel * dilation).astype(np.float32)
    bias = -slopes[:, None, None] * dist[None]
    tabs = []
    for valid_cols in (j >= BAND_HALF, j >= 0, j < Q_BLOCK + BAND_HALF):
        tabs.append(np.where((band & valid_cols)[None], bias, np.float32(NEG)))
    return np.stack(tabs).astype(np.float32)


def _attn_kernel(q_ref, kp_ref, km_ref, kn_ref, vp_ref, vm_ref, vn_ref, bias_ref,
                 o_ref, lse_ref, ke_ref, ko_ref, vf_ref, *, tq, n_blocks_total):
    lane512 = lax.broadcasted_iota(jnp.int32, (1, ATTN_WIDTH), 1)
    even_head = (lane512 % LANES) < HEAD_DIM

    def stage(rows, kblk, vblk):
        zero = jnp.zeros_like(kblk)
        ke_ref[rows, :] = jnp.where(even_head, kblk, zero)
        ko_ref[rows, :] = jnp.where(even_head, zero, kblk)
        vf_ref[rows, :] = vblk

    stage(pl.ds(0, BAND_HALF), kp_ref[0], vp_ref[0])
    stage(pl.ds(BAND_HALF, tq), km_ref[0], vm_ref[0])
    stage(pl.ds(BAND_HALF + tq, BAND_HALF), kn_ref[0], vn_ref[0])

    nb = tq // Q_BLOCK
    tile = pl.program_id(2)
    lane = lax.broadcasted_iota(jnp.int32, (1, LANES), 1)

    def block(b, carry):
        r0 = pl.multiple_of(b * Q_BLOCK, Q_BLOCK)
        gblk = tile * nb + b
        variant = jnp.where(gblk == 0, 0, jnp.where(gblk == n_blocks_total - 1, 2, 1))
        lse_tile = jnp.zeros((Q_BLOCK, LANES), F32)
        for g in range(N_HEADS // 2):
            cols = slice(g * LANES, (g + 1) * LANES)
            qp = q_ref[0, pl.ds(r0, Q_BLOCK), cols]
            vpair = vf_ref[pl.ds(r0, K_BLOCK), cols]
            outs = []
            for e, kref in enumerate((ke_ref, ko_ref)):
                h = 2 * g + e
                kk = kref[pl.ds(r0, K_BLOCK), cols]
                s = lax.dot_general(qp, kk, (((1,), (1,)), ((), ())), preferred_element_type=F32)
                s = s + bias_ref[variant, h]
                m = jnp.max(s, axis=-1, keepdims=True)
                p = jnp.exp(s - m)
                l = jnp.sum(p, axis=-1, keepdims=True)
                pv = jnp.dot(p.astype(BF16), vpair, preferred_element_type=F32)
                outs.append(pv / l)
                lse_tile = jnp.where(lane == h, m + jnp.log(l), lse_tile)
            o_ref[0, pl.ds(r0, Q_BLOCK), cols] = jnp.where(lane < HEAD_DIM, outs[0], outs[1]).astype(BF16)
        lse_ref[0, pl.ds(r0, Q_BLOCK), :] = lse_tile
        return carry

    lax.fori_loop(0, nb, block, 0)


def _band_attention(q, k, v, batch, seq, dilation):
    d = dilation
    n = seq // d
    assert n % Q_BLOCK == 0 and n >= 2 * Q_BLOCK
    tq = min(n, 1024)
    halo_per_tile = tq // BAND_HALF
    n_halo_blocks = n // BAND_HALF
    view = lambda a: a.reshape(batch, n, d * ATTN_WIDTH)
    bias = jnp.asarray(_bias_tables(d))

    main = pl.BlockSpec((1, tq, ATTN_WIDTH), lambda b, r, i: (b, i, r))
    prev = pl.BlockSpec((1, BAND_HALF, ATTN_WIDTH),
                        lambda b, r, i: (b, jnp.maximum(i * halo_per_tile - 1, 0), r))
    nxt = pl.BlockSpec((1, BAND_HALF, ATTN_WIDTH),
                       lambda b, r, i: (b, jnp.minimum((i + 1) * halo_per_tile, n_halo_blocks - 1), r))
    kv_rows = tq + 2 * BAND_HALF
    o, lse = pl.pallas_call(
        functools.partial(_attn_kernel, tq=tq, n_blocks_total=n // Q_BLOCK),
        grid=(batch, d, n // tq),
        in_specs=[main, prev, main, nxt, prev, main, nxt,
                  pl.BlockSpec((3, N_HEADS, Q_BLOCK, K_BLOCK), lambda b, r, i: (0, 0, 0, 0))],
        out_specs=[main, pl.BlockSpec((1, tq, LANES), lambda b, r, i: (b, i, r))],
        out_shape=[jax.ShapeDtypeStruct((batch, n, d * ATTN_WIDTH), BF16),
                   jax.ShapeDtypeStruct((batch, n, d * LANES), F32)],
        scratch_shapes=[pltpu.VMEM((kv_rows, ATTN_WIDTH), BF16)] * 3,
        compiler_params=pltpu.CompilerParams(dimension_semantics=("arbitrary",) * 3,
                                             vmem_limit_bytes=VMEM_LIMIT),
        name=f"band_attn_d{d}",
    )(view(q), view(k), view(k), view(k), view(v), view(v), view(v), bias)
    return o.reshape(batch * seq, ATTN_WIDTH), lse.reshape(batch * seq, LANES)


def _layernorm(x, g, b):
    mu = jnp.mean(x, axis=-1, keepdims=True)
    xc = x - mu
    var = jnp.mean(xc * xc, axis=-1, keepdims=True)
    return xc * lax.rsqrt(var + LN_EPS) * g + b


def _rmsnorm(x, g):
    return x * lax.rsqrt(jnp.mean(x * x, axis=-1, keepdims=True) + LN_EPS) * g


def _post_kernel(x_ref, o1_ref, o2_ref, o3_ref, l1_ref, l2_ref, l3_ref,
                 h_ref, hprev_ref, hnext_ref, gb_ref, convw_ref, gattn_ref, gconv_ref, wo_ref,
                 ln1g_ref, ln1b_ref, wg_ref, wu_ref, wd_ref, ln2g_ref, ln2b_ref, out_ref,
                 *, tile, tiles_per_seq):
    i = pl.program_id(0)
    lane = lax.broadcasted_iota(jnp.int32, (1, LANES), 1)

    lses = (l1_ref[...], l2_ref[...], l3_ref[...])
    top = jnp.maximum(jnp.maximum(lses[0], lses[1]), lses[2])
    es = [jnp.exp(l - top) for l in lses]
    inv = 1.0 / (es[0] + es[1] + es[2])
    ws = [e * inv for e in es]
    pairs = []
    for g in range(N_HEADS // 2):
        cols = slice(g * LANES, (g + 1) * LANES)
        acc = None
        for w, o_ref in zip(ws, (o1_ref, o2_ref, o3_ref)):
            wpair = jnp.where(lane < HEAD_DIM, w[:, 2 * g:2 * g + 1], w[:, 2 * g + 1:2 * g + 2])
            term = wpair * o_ref[:, cols].astype(F32)
            acc = term if acc is None else acc + term
        pairs.append(acc)
    attn = jnp.concatenate(pairs, axis=-1)

    hf = h_ref[...].astype(F32)
    pos = i % tiles_per_seq
    prev_row = jnp.where(pos == 0, 0.0, hprev_ref[HALO_ROWS_BF16 - 1:HALO_ROWS_BF16, :].astype(F32))
    next_row = jnp.where(pos == tiles_per_seq - 1, 0.0, hnext_ref[0:1, :].astype(F32))
    row = lax.broadcasted_iota(jnp.int32, (tile, 1), 0)
    h_m1 = jnp.where(row == 0, prev_row, pltpu.roll(hf, 1, 0))
    h_p1 = jnp.where(row == tile - 1, next_row, pltpu.roll(hf, tile - 1, 0))
    cw = convw_ref[...]
    conv = gb_ref[...].astype(F32) * (cw[0:1, :] * h_m1 + cw[1:2, :] * hf + cw[2:3, :] * h_p1)

    mix_in = jnp.concatenate([_rmsnorm(attn, gattn_ref[...]), _rmsnorm(conv, gconv_ref[...])], axis=-1)
    mix = jnp.dot(mix_in.astype(BF16), wo_ref[...], preferred_element_type=F32)
    x1 = _layernorm(ALPHA * x_ref[...] + mix, ln1g_ref[...], ln1b_ref[...])

    x1b = x1.astype(BF16)
    ffn = None
    for c0, cw_ in FFN_CHUNKS:
        gate = jnp.dot(x1b, wg_ref[:, c0:c0 + cw_], preferred_element_type=F32)
        up = jnp.dot(x1b, wu_ref[:, c0:c0 + cw_], preferred_element_type=F32)
        act = (gate * (1.0 / (1.0 + jnp.exp(-gate))) * up).astype(BF16)
        part = jnp.dot(act, wd_ref[c0:c0 + cw_, :], preferred_element_type=F32)
        ffn = part if ffn is None else ffn + part
    out_ref[...] = _layernorm(ALPHA * x1 + ffn, ln2g_ref[...], ln2b_ref[...])


def _post(x2d, os_, lses, h, gb, params, seq, tile):
    nt = x2d.shape[0]
    halo_per_tile = tile // HALO_ROWS_BF16
    n_halo = nt // HALO_ROWS_BF16
    row = lambda width: pl.BlockSpec((tile, width), lambda i: (i, 0))
    const = lambda shape: pl.BlockSpec(shape, lambda i: (0,) * len(shape), pipeline_mode=pl.Buffered(1))
    hprev = pl.BlockSpec((HALO_ROWS_BF16, CONV_WIDTH), lambda i: (jnp.maximum(i * halo_per_tile - 1, 0), 0))
    hnext = pl.BlockSpec((HALO_ROWS_BF16, CONV_WIDTH),
                         lambda i: (jnp.minimum((i + 1) * halo_per_tile, n_halo - 1), 0))
    (conv_w, g_attn, g_conv, w_o, ln1_g, ln1_b, w_gate, w_up, w_down, ln2_g, ln2_b) = params
    vec = lambda a: a.reshape(1, -1).astype(F32)
    return pl.pallas_call(
        functools.partial(_post_kernel, tile=tile, tiles_per_seq=seq // tile),
        grid=(nt // tile,),
        in_specs=[row(D_MODEL)] + [row(ATTN_WIDTH)] * 3 + [row(LANES)] * 3
                 + [row(CONV_WIDTH), hprev, hnext, row(CONV_WIDTH)]
                 + [const((3, CONV_WIDTH)), const((1, ATTN_WIDTH)), const((1, CONV_WIDTH)),
                    const((D_MODEL, D_MODEL)), const((1, D_MODEL)), const((1, D_MODEL)),
                    const((D_MODEL, D_FF)), const((D_MODEL, D_FF)), const((D_FF, D_MODEL)),
                    const((1, D_MODEL)), const((1, D_MODEL))],
        out_specs=row(D_MODEL),
        out_shape=jax.ShapeDtypeStruct((nt, D_MODEL), F32),
        compiler_params=pltpu.CompilerParams(dimension_semantics=("arbitrary",),
                                             vmem_limit_bytes=VMEM_LIMIT),
        name="post",
    )(x2d, *os_, *lses, h, h, h, gb, conv_w.astype(F32), vec(g_attn), vec(g_conv), w_o.astype(BF16),
      vec(ln1_g), vec(ln1_b), w_gate.astype(BF16), w_up.astype(BF16), w_down.astype(BF16),
      vec(ln2_g), vec(ln2_b))


def _layer(x, w_in_bf16, params):
    batch, seq, _ = x.shape
    x2d = x.reshape(batch * seq, D_MODEL)
    q, k, v, h, gb = _inproj(x2d, w_in_bf16, tile=1024)
    os_, lses = [], []
    for _, dilation in ATTN_CONFIGS:
        o, lse = _band_attention(q, k, v, batch, seq, dilation)
        os_.append(o)
        lses.append(lse)
    y = _post(x2d, os_, lses, h, gb, params, seq, tile=512)
    return y.reshape(batch, seq, D_MODEL)


def kernel(x_prompt, x_sample, w_in, conv_w, g_attn, g_conv, w_o, ln1_g, ln1_b, w_gate, w_up, w_down, ln2_g, ln2_b):
    w_in_bf16 = w_in.astype(BF16)
    params = (conv_w, g_attn, g_conv, w_o, ln1_g, ln1_b, w_gate, w_up, w_down, ln2_g, ln2_b)
    y_prompt = x_prompt
    y_sample = x_sample
    for _ in range(DEPTH):
        y_prompt = _layer(y_prompt, w_in_bf16, params)
        y_sample = _layer(y_sample, w_in_bf16, params)
    return (y_prompt, y_sample)
```

```python
import functools
import math

import numpy as np
import jax
import jax.numpy as jnp
from jax import lax
from jax.experimental import pallas as pl
from jax.experimental.pallas import tpu as pltpu

D_MODEL = 1024
ATTN_WIDTH = 512
CONV_WIDTH = 512
HEAD_DIM = 64
N_HEADS = 8
ATTN_CONFIGS = ((128, 1), (512, 4), (2048, 16))
BAND_HALF = 64
D_FF = 2816
DEPTH = 1
ALPHA = (2.0 * DEPTH) ** 0.25
LN_EPS = 1e-5
NEG = -1e30
LOG2E = math.log2(math.e)
LN2 = math.log(2.0)

LANES = 128
SLABS = ATTN_WIDTH // LANES
Q_BLOCK = 128
K_BLOCK = Q_BLOCK + 2 * BAND_HALF
HALO_ROWS_BF16 = 16
FFN_CHUNKS = ((0, 1024), (1024, 1024), (2048, 768))
VMEM_LIMIT = 56 * 1024 * 1024

F32 = jnp.float32
BF16 = jnp.bfloat16


def _inproj_kernel(x_ref, w_ref, q1_ref, k1_ref, v1_ref, q4_ref, k4_ref, v4_ref,
                   q16_ref, k16_ref, v16_ref, h_ref, gb_ref, nat_scr, d4_scr, *, tile):
    xb = x_ref[...].astype(BF16)

    def proj(c):
        return jnp.dot(xb, w_ref[:, c * 512:(c + 1) * 512], preferred_element_type=F32)

    def emit(val, o1_ref, o4_ref, o16_ref):
        o1_ref[...] = val.astype(BF16)
        q4, q16 = tile // 4, tile // 16
        for s in range(SLABS):
            lanes = slice(s * LANES, (s + 1) * LANES)
            nat_scr[s] = val[:, lanes]
            for r1 in range(4):
                plane = nat_scr[s, pl.ds(r1, q4, stride=4), :]
                d4_scr[s, r1 * q4:(r1 + 1) * q4, :] = plane
                o4_ref[0, r1, :, lanes] = plane.astype(BF16)
            for r1 in range(4):
                for r2 in range(4):
                    plane = d4_scr[s, pl.ds(r1 * q4 + r2, q16, stride=4), :]
                    o16_ref[0, 4 * r2 + r1, :, lanes] = plane.astype(BF16)

    emit(proj(0) * (LOG2E / math.sqrt(HEAD_DIM)), q1_ref, q4_ref, q16_ref)
    emit(proj(1), k1_ref, k4_ref, k16_ref)
    emit(proj(2), v1_ref, v4_ref, v16_ref)
    u = proj(3)
    gb_ref[...] = proj(4).astype(BF16)
    h_ref[...] = (proj(5) * u).astype(BF16)


def _inproj(x2d, w_in_bf16, batch, seq, tile):
    nt = x2d.shape[0]
    tps = seq // tile
    nat = jax.ShapeDtypeStruct((nt, 512), BF16)
    row_spec = pl.BlockSpec((tile, 512), lambda i: (i, 0))

    def plane_out(d):
        shape = jax.ShapeDtypeStruct((batch, d, seq // d, ATTN_WIDTH), BF16)
        spec = pl.BlockSpec((1, d, tile // d, ATTN_WIDTH), lambda i: (i // tps, 0, i % tps, 0))
        return shape, spec

    s4, b4 = plane_out(4)
    s16, b16 = plane_out(16)
    return pl.pallas_call(
        functools.partial(_inproj_kernel, tile=tile),
        grid=(nt // tile,),
        in_specs=[pl.BlockSpec((tile, D_MODEL), lambda i: (i, 0)),
                  pl.BlockSpec((D_MODEL, 6 * 512), lambda i: (0, 0))],
        out_specs=[row_spec] * 3 + [b4] * 3 + [b16] * 3 + [row_spec] * 2,
        out_shape=[nat] * 3 + [s4] * 3 + [s16] * 3 + [nat] * 2,
        scratch_shapes=[pltpu.VMEM((SLABS, tile, LANES), F32)] * 2,
        compiler_params=pltpu.CompilerParams(dimension_semantics=("arbitrary",),
                                             vmem_limit_bytes=VMEM_LIMIT),
        name="inproj",
    )(x2d, w_in_bf16)


def _bias_tables(dilation):
    slopes = np.array([2.0 ** (-8.0 * (h + 1) / N_HEADS) for h in range(N_HEADS)], np.float64)
    i = np.arange(Q_BLOCK)[:, None]
    j = np.arange(K_BLOCK)[None, :]
    rel = np.abs(i + BAND_HALF - j)
    band = rel <= BAND_HALF
    dist = (rel * dilation).astype(np.float64)
    bias = -slopes[:, None, None] * dist[None] * LOG2E
    tabs = []
    for valid_cols in (j >= BAND_HALF, j >= 0, j < Q_BLOCK + BAND_HALF):
        tabs.append(np.where((band & valid_cols)[None], bias, NEG))
    return np.stack(tabs).astype(np.float32)


def _attn_kernel(q_ref, kp_ref, km_ref, kn_ref, vp_ref, vm_ref, vn_ref, bias_ref,
                 o_ref, lse_ref, *, tq, n_tiles):
    nb = tq // Q_BLOCK
    tile = pl.program_id(2)
    lane = lax.broadcasted_iota(jnp.int32, (1, LANES), 1)
    low = lane < HEAD_DIM

    def chunk(prev_ref, main_ref, next_ref, c):
        if c == 0:
            return jnp.concatenate([prev_ref[0, 0], main_ref[0, 0, 0:BAND_HALF, :]], axis=0)
        if c == nb:
            return jnp.concatenate([main_ref[0, 0, tq - BAND_HALF:tq, :], next_ref[0, 0]], axis=0)
        return main_ref[0, 0, c * Q_BLOCK - BAND_HALF:c * Q_BLOCK + BAND_HALF, :]

    kt, vc = [], []
    for c in range(nb + 1):
        kc = chunk(kp_ref, km_ref, kn_ref, c)
        kt.append([kc[:, g * LANES:(g + 1) * LANES].T for g in range(SLABS)])
        vc.append(chunk(vp_ref, vm_ref, vn_ref, c))
    ones = jnp.ones((K_BLOCK, LANES), BF16)

    for b in range(nb):
        rows = slice(b * Q_BLOCK, (b + 1) * Q_BLOCK)
        if b == 0:
            variant = jnp.where(tile == 0, 0, 1)
        elif b == nb - 1:
            variant = jnp.where(tile == n_tiles - 1, 2, 1)
        else:
            variant = 1
        lse_tile = jnp.zeros((Q_BLOCK, LANES), F32)
        for g in range(SLABS):
            cols = slice(g * LANES, (g + 1) * LANES)
            qp = q_ref[0, 0, rows, cols]
            ktp = jnp.concatenate([kt[b][g], kt[b + 1][g]], axis=1)
            vpair = jnp.concatenate([vc[b][:, cols], vc[b + 1][:, cols]], axis=0)
            vaug = jnp.concatenate([vpair, ones], axis=1)
            outs = []
            for e in range(2):
                h = 2 * g + e
                qe = jnp.where(low if e == 0 else jnp.logical_not(low), qp, jnp.zeros_like(qp))
                s = jnp.dot(qe, ktp, preferred_element_type=F32) + bias_ref[variant, h]
                m = jnp.max(s, axis=-1, keepdims=True)
                p = jnp.exp2(s - m)
                pv = jnp.dot(p.astype(BF16), vaug, preferred_element_type=F32)
                l = pv[:, LANES:]
                outs.append(pv[:, :LANES] / l)
                lse_tile = jnp.where(lane == h, (m + jnp.log2(l[:, 0:1])) * LN2, lse_tile)
            o_ref[0, 0, rows, cols] = jnp.where(low, outs[0], outs[1]).astype(BF16)
        lse_ref[0, 0, rows, :] = lse_tile


def _band_attention(q, k, v, dilation):
    batch, d, n, _ = q.shape
    assert d == dilation and n % Q_BLOCK == 0 and n >= 2 * Q_BLOCK
    tq = min(n, 1024)
    halo_per_tile = tq // BAND_HALF
    n_halo_blocks = n // BAND_HALF
    bias = jnp.asarray(_bias_tables(d))

    main = pl.BlockSpec((1, 1, tq, ATTN_WIDTH), lambda b, r, i: (b, r, i, 0))
    prev = pl.BlockSpec((1, 1, BAND_HALF, ATTN_WIDTH),
                        lambda b, r, i: (b, r, jnp.maximum(i * halo_per_tile - 1, 0), 0))
    nxt = pl.BlockSpec((1, 1, BAND_HALF, ATTN_WIDTH),
                       lambda b, r, i: (b, r, jnp.minimum((i + 1) * halo_per_tile, n_halo_blocks - 1), 0))
    return pl.pallas_call(
        functools.partial(_attn_kernel, tq=tq, n_tiles=n // tq),
        grid=(batch, d, n // tq),
        in_specs=[main, prev, main, nxt, prev, main, nxt,
                  pl.BlockSpec((3, N_HEADS, Q_BLOCK, K_BLOCK), lambda b, r, i: (0, 0, 0, 0))],
        out_specs=[main, pl.BlockSpec((1, 1, tq, LANES), lambda b, r, i: (b, r, i, 0))],
        out_shape=[jax.ShapeDtypeStruct((batch, d, n, ATTN_WIDTH), BF16),
                   jax.ShapeDtypeStruct((batch, d, n, LANES), F32)],
        compiler_params=pltpu.CompilerParams(dimension_semantics=("arbitrary",) * 3,
                                             vmem_limit_bytes=VMEM_LIMIT),
        name=f"band_attn_d{d}",
    )(q, k, k, k, v, v, v, bias)


def _layernorm(x, g, b):
    mu = jnp.mean(x, axis=-1, keepdims=True)
    xc = x - mu
    var = jnp.mean(xc * xc, axis=-1, keepdims=True)
    return xc * lax.rsqrt(var + LN_EPS) * g + b


def _rmsnorm(x, g):
    return x * lax.rsqrt(jnp.mean(x * x, axis=-1, keepdims=True) + LN_EPS) * g


def _post_kernel(x_ref, o1_ref, o4_ref, o16_ref, l1_ref, l4_ref, l16_ref,
                 h_ref, hprev_ref, hnext_ref, gb_ref, convw_ref, gattn_ref, gconv_ref, wo_ref,
                 ln1g_ref, ln1b_ref, wg_ref, wu_ref, wd_ref, ln2g_ref, ln2b_ref, out_ref,
                 o4_scr, o16_scr, l4_scr, l16_scr, *, tile, tiles_per_seq):
    i = pl.program_id(0)
    lane = lax.broadcasted_iota(jnp.int32, (1, LANES), 1)

    def interleave(src_ref, dst_scr, d, slabs):
        for r in range(d):
            for s in range(slabs):
                dst_scr[s, pl.ds(r, tile // d, stride=d), :] = \
                    src_ref[0, r, :, s * LANES:(s + 1) * LANES].astype(F32)

    interleave(o4_ref, o4_scr, 4, SLABS)
    interleave(o16_ref, o16_scr, 16, SLABS)
    interleave(l4_ref, l4_scr, 4, 1)
    interleave(l16_ref, l16_scr, 16, 1)

    lses = (l1_ref[...], l4_scr[0], l16_scr[0])
    top = jnp.maximum(jnp.maximum(lses[0], lses[1]), lses[2])
    es = [jnp.exp(l - top) for l in lses]
    inv = 1.0 / (es[0] + es[1] + es[2])
    ws = [e * inv for e in es]
    pairs = []
    for g in range(SLABS):
        cols = slice(g * LANES, (g + 1) * LANES)
        os_ = (o1_ref[:, cols].astype(F32), o4_scr[g], o16_scr[g])
        acc = None
        for w, o in zip(ws, os_):
            wpair = jnp.where(lane < HEAD_DIM, w[:, 2 * g:2 * g + 1], w[:, 2 * g + 1:2 * g + 2])
            acc = wpair * o if acc is None else acc + wpair * o
        pairs.append(acc)
    attn = jnp.concatenate(pairs, axis=-1)

    hf = h_ref[...].astype(F32)
    pos = i % tiles_per_seq
    prev_row = jnp.where(pos == 0, 0.0, hprev_ref[HALO_ROWS_BF16 - 1:HALO_ROWS_BF16, :].astype(F32))
    next_row = jnp.where(pos == tiles_per_seq - 1, 0.0, hnext_ref[0:1, :].astype(F32))
    row = lax.broadcasted_iota(jnp.int32, (tile, 1), 0)
    h_m1 = jnp.where(row == 0, prev_row, pltpu.roll(hf, 1, 0))
    h_p1 = jnp.where(row == tile - 1, next_row, pltpu.roll(hf, tile - 1, 0))
    cw = convw_ref[...]
    conv = gb_ref[...].astype(F32) * (cw[0:1, :] * h_m1 + cw[1:2, :] * hf + cw[2:3, :] * h_p1)

    mix_in = jnp.concatenate([_rmsnorm(attn, gattn_ref[...]), _rmsnorm(conv, gconv_ref[...])], axis=-1)
    mix = jnp.dot(mix_in.astype(BF16), wo_ref[...], preferred_element_type=F32)
    x1 = _layernorm(ALPHA * x_ref[...] + mix, ln1g_ref[...], ln1b_ref[...])

    x1b = x1.astype(BF16)
    ffn = None
    for c0, cw_ in FFN_CHUNKS:
        gate = jnp.dot(x1b, wg_ref[:, c0:c0 + cw_], preferred_element_type=F32)
        up = jnp.dot(x1b, wu_ref[:, c0:c0 + cw_], preferred_element_type=F32)
        act = (gate * (1.0 / (1.0 + jnp.exp(-gate))) * up).astype(BF16)
        part = jnp.dot(act, wd_ref[c0:c0 + cw_, :], preferred_element_type=F32)
        ffn = part if ffn is None else ffn + part
    out_ref[...] = _layernorm(ALPHA * x1 + ffn, ln2g_ref[...], ln2b_ref[...])


def _post(x2d, os_, lses, h, gb, params, seq, tile):
    nt = x2d.shape[0]
    tps = seq // tile
    halo_per_tile = tile // HALO_ROWS_BF16
    n_halo = nt // HALO_ROWS_BF16
    row = lambda width: pl.BlockSpec((tile, width), lambda i: (i, 0))
    planes = lambda d, width: pl.BlockSpec((1, d, tile // d, width), lambda i: (i // tps, 0, i % tps, 0))
    const = lambda shape: pl.BlockSpec(shape, lambda i: (0,) * len(shape), pipeline_mode=pl.Buffered(1))
    hprev = pl.BlockSpec((HALO_ROWS_BF16, CONV_WIDTH), lambda i: (jnp.maximum(i * halo_per_tile - 1, 0), 0))
    hnext = pl.BlockSpec((HALO_ROWS_BF16, CONV_WIDTH),
                         lambda i: (jnp.minimum((i + 1) * halo_per_tile, n_halo - 1), 0))
    (conv_w, g_attn, g_conv, w_o, ln1_g, ln1_b, w_gate, w_up, w_down, ln2_g, ln2_b) = params
    vec = lambda a: a.reshape(1, -1).astype(F32)
    return pl.pallas_call(
        functools.partial(_post_kernel, tile=tile, tiles_per_seq=tps),
        grid=(nt // tile,),
        in_specs=[row(D_MODEL), row(ATTN_WIDTH), planes(4, ATTN_WIDTH), planes(16, ATTN_WIDTH),
                  row(LANES), planes(4, LANES), planes(16, LANES),
                  row(CONV_WIDTH), hprev, hnext, row(CONV_WIDTH),
                  const((3, CONV_WIDTH)), const((1, ATTN_WIDTH)), const((1, CONV_WIDTH)),
                  const((D_MODEL, D_MODEL)), const((1, D_MODEL)), const((1, D_MODEL)),
                  const((D_MODEL, D_FF)), const((D_MODEL, D_FF)), const((D_FF, D_MODEL)),
                  const((1, D_MODEL)), const((1, D_MODEL))],
        out_specs=row(D_MODEL),
        out_shape=jax.ShapeDtypeStruct((nt, D_MODEL), F32),
        scratch_shapes=[pltpu.VMEM((SLABS, tile, LANES), F32), pltpu.VMEM((SLABS, tile, LANES), F32),
                        pltpu.VMEM((1, tile, LANES), F32), pltpu.VMEM((1, tile, LANES), F32)],
        compiler_params=pltpu.CompilerParams(dimension_semantics=("arbitrary",),
                                             vmem_limit_bytes=VMEM_LIMIT),
        name="post",
    )(x2d, *os_, *lses, h, h, h, gb, conv_w.astype(F32), vec(g_attn), vec(g_conv), w_o.astype(BF16),
      vec(ln1_g), vec(ln1_b), w_gate.astype(BF16), w_up.astype(BF16), w_down.astype(BF16),
      vec(ln2_g), vec(ln2_b))


def _layer(x, w_in_bf16, params):
    batch, seq, _ = x.shape
    nt = batch * seq
    x2d = x.reshape(nt, D_MODEL)
    q1, k1, v1, q4, k4, v4, q16, k16, v16, h, gb = _inproj(x2d, w_in_bf16, batch, seq, tile=1024)
    as_planes = lambda a: a.reshape(batch, 1, seq, ATTN_WIDTH)
    o1, l1 = _band_attention(as_planes(q1), as_planes(k1), as_planes(v1), 1)
    o4, l4 = _band_attention(q4, k4, v4, 4)
    o16, l16 = _band_attention(q16, k16, v16, 16)
    y = _post(x2d, (o1.reshape(nt, ATTN_WIDTH), o4, o16), (l1.reshape(nt, LANES), l4, l16),
              h, gb, params, seq, tile=512)
    return y.reshape(batch, seq, D_MODEL)


def kernel(x_prompt, x_sample, w_in, conv_w, g_attn, g_conv, w_o, ln1_g, ln1_b, w_gate, w_up, w_down, ln2_g, ln2_b):
    w_in_bf16 = w_in.astype(BF16)
    params = (conv_w, g_attn, g_conv, w_o, ln1_g, ln1_b, w_gate, w_up, w_down, ln2_g, ln2_b)
    y_prompt = x_prompt
    y_sample = x_sample
    for _ in range(DEPTH):
        y_prompt = _layer(y_prompt, w_in_bf16, params)
        y_sample = _layer(y_sample, w_in_bf16, params)
    return (y_prompt, y_sample)
```

```python
import functools
import math

import numpy as np
import jax
import jax.numpy as jnp
from jax import lax
from jax.experimental import pallas as pl
from jax.experimental.pallas import tpu as pltpu

D_MODEL = 1024
ATTN_WIDTH = 512
CONV_WIDTH = 512
HEAD_DIM = 64
N_HEADS = 8
ATTN_CONFIGS = ((128, 1), (512, 4), (2048, 16))
BAND_HALF = 64
D_FF = 2816
DEPTH = 1
ALPHA = (2.0 * DEPTH) ** 0.25
LN_EPS = 1e-5
NEG = -1e30
LOG2E = math.log2(math.e)
LN2 = math.log(2.0)

LANES = 128
SLABS = ATTN_WIDTH // LANES
Q_BLOCK = 128
K_BLOCK = Q_BLOCK + 2 * BAND_HALF
HALO_ROWS_BF16 = 16
FFN_CHUNKS = ((0, 1024), (1024, 1024), (2048, 768))
VMEM_LIMIT = 56 * 1024 * 1024

F32 = jnp.float32
BF16 = jnp.bfloat16


def _inproj_kernel(x_ref, w_ref, q1_ref, k1_ref, v1_ref, q4_ref, k4_ref, v4_ref,
                   q16_ref, k16_ref, v16_ref, h_ref, gb_ref, nat_scr, d4_scr, *, tile):
    xb = x_ref[...].astype(BF16)

    def proj(c):
        return jnp.dot(xb, w_ref[:, c * 512:(c + 1) * 512], preferred_element_type=F32)

    def emit(val, o1_ref, o4_ref, o16_ref):
        o1_ref[...] = val.astype(BF16)
        q4, q16 = tile // 4, tile // 16
        for s in range(SLABS):
            lanes = slice(s * LANES, (s + 1) * LANES)
            nat_scr[s] = val[:, lanes]
            for r1 in range(4):
                plane = nat_scr[s, pl.ds(r1, q4, stride=4), :]
                d4_scr[s, r1 * q4:(r1 + 1) * q4, :] = plane
                o4_ref[0, r1, :, lanes] = plane.astype(BF16)
            for r1 in range(4):
                for r2 in range(4):
                    plane = d4_scr[s, pl.ds(r1 * q4 + r2, q16, stride=4), :]
                    o16_ref[0, 4 * r2 + r1, :, lanes] = plane.astype(BF16)

    emit(proj(0) * (LOG2E / math.sqrt(HEAD_DIM)), q1_ref, q4_ref, q16_ref)
    emit(proj(1), k1_ref, k4_ref, k16_ref)
    emit(proj(2), v1_ref, v4_ref, v16_ref)
    u = proj(3)
    gb_ref[...] = proj(4).astype(BF16)
    h_ref[...] = (proj(5) * u).astype(BF16)


def _inproj(x2d, w_in_bf16, batch, seq, tile):
    nt = x2d.shape[0]
    tps = seq // tile
    nat = jax.ShapeDtypeStruct((nt, 512), BF16)
    row_spec = pl.BlockSpec((tile, 512), lambda i: (i, 0))

    def plane_out(d):
        shape = jax.ShapeDtypeStruct((batch, d, seq // d, ATTN_WIDTH), BF16)
        spec = pl.BlockSpec((1, d, tile // d, ATTN_WIDTH), lambda i: (i // tps, 0, i % tps, 0))
        return shape, spec

    s4, b4 = plane_out(4)
    s16, b16 = plane_out(16)
    return pl.pallas_call(
        functools.partial(_inproj_kernel, tile=tile),
        grid=(nt // tile,),
        in_specs=[pl.BlockSpec((tile, D_MODEL), lambda i: (i, 0)),
                  pl.BlockSpec((D_MODEL, 6 * 512), lambda i: (0, 0))],
        out_specs=[row_spec] * 3 + [b4] * 3 + [b16] * 3 + [row_spec] * 2,
        out_shape=[nat] * 3 + [s4] * 3 + [s16] * 3 + [nat] * 2,
        scratch_shapes=[pltpu.VMEM((SLABS, tile, LANES), F32)] * 2,
        compiler_params=pltpu.CompilerParams(dimension_semantics=("arbitrary",),
                                             vmem_limit_bytes=VMEM_LIMIT),
        name="inproj",
    )(x2d, w_in_bf16)


def _bias_tables(dilation):
    slopes = np.array([2.0 ** (-8.0 * (h + 1) / N_HEADS) for h in range(N_HEADS)], np.float64)
    i = np.arange(Q_BLOCK)[:, None]
    j = np.arange(K_BLOCK)[None, :]
    rel = np.abs(i + BAND_HALF - j)
    band = rel <= BAND_HALF
    dist = (rel * dilation).astype(np.float64)
    bias = -slopes[:, None, None] * dist[None] * LOG2E
    tabs = []
    for valid_cols in (j >= BAND_HALF, j >= 0, j < Q_BLOCK + BAND_HALF):
        tabs.append(np.where((band & valid_cols)[None], bias, NEG))
    return np.stack(tabs).astype(np.float32)


def _attn_kernel(q_ref, kp_ref, km_ref, kn_ref, vp_ref, vm_ref, vn_ref, bias_ref,
                 o_ref, lse_ref, *, tq, n_tiles):
    nb = tq // Q_BLOCK
    tile = pl.program_id(2)
    lane = lax.broadcasted_iota(jnp.int32, (1, LANES), 1)
    low = lane < HEAD_DIM

    def chunk(prev_ref, main_ref, next_ref, c):
        if c == 0:
            return jnp.concatenate([prev_ref[0, 0], main_ref[0, 0, 0:BAND_HALF, :]], axis=0)
        if c == nb:
            return jnp.concatenate([main_ref[0, 0, tq - BAND_HALF:tq, :], next_ref[0, 0]], axis=0)
        return main_ref[0, 0, c * Q_BLOCK - BAND_HALF:c * Q_BLOCK + BAND_HALF, :]

    kt, vc = [], []
    for c in range(nb + 1):
        kc = chunk(kp_ref, km_ref, kn_ref, c)
        kt.append([kc[:, g * LANES:(g + 1) * LANES].T for g in range(SLABS)])
        vc.append(chunk(vp_ref, vm_ref, vn_ref, c))
    ones = jnp.ones((K_BLOCK, LANES), BF16)

    for b in range(nb):
        rows = slice(b * Q_BLOCK, (b + 1) * Q_BLOCK)
        if b == 0:
            variant = jnp.where(tile == 0, 0, 1)
        elif b == nb - 1:
            variant = jnp.where(tile == n_tiles - 1, 2, 1)
        else:
            variant = 1
        m_tile = jnp.zeros((Q_BLOCK, LANES), F32)
        l_tile = jnp.ones((Q_BLOCK, LANES), F32)
        for g in range(SLABS):
            cols = slice(g * LANES, (g + 1) * LANES)
            qp = q_ref[0, 0, rows, cols]
            ktp = jnp.concatenate([kt[b][g], kt[b + 1][g]], axis=1)
            vpair = jnp.concatenate([vc[b][:, cols], vc[b + 1][:, cols]], axis=0)
            vaug = jnp.concatenate([vpair, ones], axis=1)
            outs = []
            for e in range(2):
                h = 2 * g + e
                qe = jnp.where(low if e == 0 else jnp.logical_not(low), qp, jnp.zeros_like(qp))
                s = jnp.dot(qe, ktp, preferred_element_type=F32) + bias_ref[variant, h]
                m = jnp.max(s, axis=-1, keepdims=True)
                p = jnp.exp2(s - m)
                pv = jnp.dot(p.astype(BF16), vaug, preferred_element_type=F32)
                l = pv[:, LANES:]
                outs.append(pv[:, :LANES] / l)
                m_tile = jnp.where(lane == h, m, m_tile)
                l_tile = jnp.where(lane == h, l, l_tile)
            o_ref[0, 0, rows, cols] = jnp.where(low, outs[0], outs[1]).astype(BF16)
        lse_ref[0, 0, rows, :] = (m_tile + jnp.log2(l_tile)) * LN2


def _band_attention(q, k, v, dilation):
    batch, d, n, _ = q.shape
    assert d == dilation and n % Q_BLOCK == 0 and n >= 2 * Q_BLOCK
    tq = min(n, 1024)
    halo_per_tile = tq // BAND_HALF
    n_halo_blocks = n // BAND_HALF
    bias = jnp.asarray(_bias_tables(d))

    main = pl.BlockSpec((1, 1, tq, ATTN_WIDTH), lambda b, r, i: (b, r, i, 0))
    prev = pl.BlockSpec((1, 1, BAND_HALF, ATTN_WIDTH),
                        lambda b, r, i: (b, r, jnp.maximum(i * halo_per_tile - 1, 0), 0))
    nxt = pl.BlockSpec((1, 1, BAND_HALF, ATTN_WIDTH),
                       lambda b, r, i: (b, r, jnp.minimum((i + 1) * halo_per_tile, n_halo_blocks - 1), 0))
    return pl.pallas_call(
        functools.partial(_attn_kernel, tq=tq, n_tiles=n // tq),
        grid=(batch, d, n // tq),
        in_specs=[main, prev, main, nxt, prev, main, nxt,
                  pl.BlockSpec((3, N_HEADS, Q_BLOCK, K_BLOCK), lambda b, r, i: (0, 0, 0, 0))],
        out_specs=[main, pl.BlockSpec((1, 1, tq, LANES), lambda b, r, i: (b, r, i, 0))],
        out_shape=[jax.ShapeDtypeStruct((batch, d, n, ATTN_WIDTH), BF16),
                   jax.ShapeDtypeStruct((batch, d, n, LANES), F32)],
        compiler_params=pltpu.CompilerParams(dimension_semantics=("arbitrary",) * 3,
                                             vmem_limit_bytes=VMEM_LIMIT),
        name=f"band_attn_d{d}",
    )(q, k, k, k, v, v, v, bias)


def _layernorm(x, g, b):
    mu = jnp.mean(x, axis=-1, keepdims=True)
    xc = x - mu
    var = jnp.mean(xc * xc, axis=-1, keepdims=True)
    return xc * lax.rsqrt(var + LN_EPS) * g + b


def _rmsnorm(x, g):
    return x * lax.rsqrt(jnp.mean(x * x, axis=-1, keepdims=True) + LN_EPS) * g


def _post_kernel(x_ref, o1_ref, o4_ref, o16_ref, l1_ref, l4_ref, l16_ref,
                 h_ref, hprev_ref, hnext_ref, gb_ref, convw_ref, gattn_ref, gconv_ref, wo_ref,
                 ln1g_ref, ln1b_ref, wg_ref, wu_ref, wd_ref, ln2g_ref, ln2b_ref, out_ref,
                 mix_scr, o4_scr, o16_scr, l4_scr, l16_scr, *, tile, tiles_per_seq, n_tiles):
    j = pl.program_id(0)

    @pl.when(j == 0)
    def _():
        mix_scr[...] = jnp.zeros_like(mix_scr)

    mix = jnp.dot(mix_scr[...], wo_ref[...], preferred_element_type=F32)
    x1 = _layernorm(ALPHA * x_ref[...] + mix, ln1g_ref[...], ln1b_ref[...])
    x1b = x1.astype(BF16)
    ffn = None
    for c0, cw_ in FFN_CHUNKS:
        gate = jnp.dot(x1b, wg_ref[:, c0:c0 + cw_], preferred_element_type=F32)
        up = jnp.dot(x1b, wu_ref[:, c0:c0 + cw_], preferred_element_type=F32)
        act = (gate * (1.0 / (1.0 + jnp.exp(-gate))) * up).astype(BF16)
        part = jnp.dot(act, wd_ref[c0:c0 + cw_, :], preferred_element_type=F32)
        ffn = part if ffn is None else ffn + part
    out_ref[...] = _layernorm(ALPHA * x1 + ffn, ln2g_ref[...], ln2b_ref[...])

    lane = lax.broadcasted_iota(jnp.int32, (1, LANES), 1)

    def interleave(src_ref, dst_scr, d, slabs):
        for r in range(d):
            for s in range(slabs):
                dst_scr[s, pl.ds(r, tile // d, stride=d), :] = \
                    src_ref[0, r, :, s * LANES:(s + 1) * LANES].astype(F32)

    interleave(o4_ref, o4_scr, 4, SLABS)
    interleave(o16_ref, o16_scr, 16, SLABS)
    interleave(l4_ref, l4_scr, 4, 1)
    interleave(l16_ref, l16_scr, 16, 1)

    lses = (l1_ref[...], l4_scr[0], l16_scr[0])
    top = jnp.maximum(jnp.maximum(lses[0], lses[1]), lses[2])
    es = [jnp.exp(l - top) for l in lses]
    inv = 1.0 / (es[0] + es[1] + es[2])
    ws = [e * inv for e in es]
    pairs = []
    for g in range(SLABS):
        cols = slice(g * LANES, (g + 1) * LANES)
        os_ = (o1_ref[:, cols].astype(F32), o4_scr[g], o16_scr[g])
        acc = None
        for w, o in zip(ws, os_):
            wpair = jnp.where(lane < HEAD_DIM, w[:, 2 * g:2 * g + 1], w[:, 2 * g + 1:2 * g + 2])
            acc = wpair * o if acc is None else acc + wpair * o
        pairs.append(acc)
    attn = jnp.concatenate(pairs, axis=-1)

    hf = h_ref[...].astype(F32)
    pos = jnp.minimum(j, n_tiles - 1) % tiles_per_seq
    prev_row = jnp.where(pos == 0, 0.0, hprev_ref[HALO_ROWS_BF16 - 1:HALO_ROWS_BF16, :].astype(F32))
    next_row = jnp.where(pos == tiles_per_seq - 1, 0.0, hnext_ref[0:1, :].astype(F32))
    row = lax.broadcasted_iota(jnp.int32, (tile, 1), 0)
    h_m1 = jnp.where(row == 0, prev_row, pltpu.roll(hf, 1, 0))
    h_p1 = jnp.where(row == tile - 1, next_row, pltpu.roll(hf, tile - 1, 0))
    cw = convw_ref[...]
    conv = gb_ref[...].astype(F32) * (cw[0:1, :] * h_m1 + cw[1:2, :] * hf + cw[2:3, :] * h_p1)

    mix_scr[:, 0:ATTN_WIDTH] = _rmsnorm(attn, gattn_ref[...]).astype(BF16)
    mix_scr[:, ATTN_WIDTH:D_MODEL] = _rmsnorm(conv, gconv_ref[...]).astype(BF16)


def _post(x2d, os_, lses, h, gb, params, seq, tile):
    nt = x2d.shape[0]
    n_tiles = nt // tile
    tps = seq // tile
    halo_per_tile = tile // HALO_ROWS_BF16
    n_halo = nt // HALO_ROWS_BF16
    cur = lambda j: jnp.minimum(j, n_tiles - 1)
    old = lambda j: jnp.maximum(j - 1, 0)
    row = lambda width: pl.BlockSpec((tile, width), lambda j: (cur(j), 0))
    planes = lambda d, width: pl.BlockSpec((1, d, tile // d, width),
                                           lambda j: (cur(j) // tps, 0, cur(j) % tps, 0))
    const = lambda shape: pl.BlockSpec(shape, lambda j: (0,) * len(shape), pipeline_mode=pl.Buffered(1))
    hprev = pl.BlockSpec((HALO_ROWS_BF16, CONV_WIDTH),
                         lambda j: (jnp.maximum(cur(j) * halo_per_tile - 1, 0), 0))
    hnext = pl.BlockSpec((HALO_ROWS_BF16, CONV_WIDTH),
                         lambda j: (jnp.minimum((cur(j) + 1) * halo_per_tile, n_halo - 1), 0))
    (conv_w, g_attn, g_conv, w_o, ln1_g, ln1_b, w_gate, w_up, w_down, ln2_g, ln2_b) = params
    vec = lambda a: a.reshape(1, -1).astype(F32)
    return pl.pallas_call(
        functools.partial(_post_kernel, tile=tile, tiles_per_seq=tps, n_tiles=n_tiles),
        grid=(n_tiles + 1,),
        in_specs=[pl.BlockSpec((tile, D_MODEL), lambda j: (old(j), 0)),
                  row(ATTN_WIDTH), planes(4, ATTN_WIDTH), planes(16, ATTN_WIDTH),
                  row(LANES), planes(4, LANES), planes(16, LANES),
                  row(CONV_WIDTH), hprev, hnext, row(CONV_WIDTH),
                  const((3, CONV_WIDTH)), const((1, ATTN_WIDTH)), const((1, CONV_WIDTH)),
                  const((D_MODEL, D_MODEL)), const((1, D_MODEL)), const((1, D_MODEL)),
                  const((D_MODEL, D_FF)), const((D_MODEL, D_FF)), const((D_FF, D_MODEL)),
                  const((1, D_MODEL)), const((1, D_MODEL))],
        out_specs=pl.BlockSpec((tile, D_MODEL), lambda j: (old(j), 0)),
        out_shape=jax.ShapeDtypeStruct((nt, D_MODEL), F32),
        scratch_shapes=[pltpu.VMEM((tile, D_MODEL), BF16),
                        pltpu.VMEM((SLABS, tile, LANES), F32), pltpu.VMEM((SLABS, tile, LANES), F32),
                        pltpu.VMEM((1, tile, LANES), F32), pltpu.VMEM((1, tile, LANES), F32)],
        compiler_params=pltpu.CompilerParams(dimension_semantics=("arbitrary",),
                                             vmem_limit_bytes=VMEM_LIMIT),
        name="post",
    )(x2d, *os_, *lses, h, h, h, gb, conv_w.astype(F32), vec(g_attn), vec(g_conv), w_o.astype(BF16),
      vec(ln1_g), vec(ln1_b), w_gate.astype(BF16), w_up.astype(BF16), w_down.astype(BF16),
      vec(ln2_g), vec(ln2_b))


def _layer(x, w_in_bf16, params):
    batch, seq, _ = x.shape
    nt = batch * seq
    x2d = x.reshape(nt, D_MODEL)
    q1, k1, v1, q4, k4, v4, q16, k16, v16, h, gb = _inproj(x2d, w_in_bf16, batch, seq, tile=1024)
    as_planes = lambda a: a.reshape(batch, 1, seq, ATTN_WIDTH)
    o1, l1 = _band_attention(as_planes(q1), as_planes(k1), as_planes(v1), 1)
    o4, l4 = _band_attention(q4, k4, v4, 4)
    o16, l16 = _band_attention(q16, k16, v16, 16)
    y = _post(x2d, (o1.reshape(nt, ATTN_WIDTH), o4, o16), (l1.reshape(nt, LANES), l4, l16),
              h, gb, params, seq, tile=512)
    return y.reshape(batch, seq, D_MODEL)


def kernel(x_prompt, x_sample, w_in, conv_w, g_attn, g_conv, w_o, ln1_g, ln1_b, w_gate, w_up, w_down, ln2_g, ln2_b):
    w_in_bf16 = w_in.astype(BF16)
    params = (conv_w, g_attn, g_conv, w_o, ln1_g, ln1_b, w_gate, w_up, w_down, ln2_g, ln2_b)
    y_prompt = x_prompt
    y_sample = x_sample
    for _ in range(DEPTH):
        y_prompt = _layer(y_prompt, w_in_bf16, params)
        y_sample = _layer(y_sample, w_in_bf16, params)
    return (y_prompt, y_sample)
```

```python
import functools
import math

import numpy as np
import jax
import jax.numpy as jnp
from jax import lax
from jax.experimental import pallas as pl
from jax.experimental.pallas import tpu as pltpu

D_MODEL = 1024
ATTN_WIDTH = 512
CONV_WIDTH = 512
HEAD_DIM = 64
N_HEADS = 8
ATTN_CONFIGS = ((128, 1), (512, 4), (2048, 16))
BAND_HALF = 64
D_FF = 2816
DEPTH = 1
ALPHA = (2.0 * DEPTH) ** 0.25
LN_EPS = 1e-5
NEG = -1e30
LOG2E = math.log2(math.e)
LN2 = math.log(2.0)

LANES = 128
SLABS = ATTN_WIDTH // LANES
Q_BLOCK = 128
K_BLOCK = Q_BLOCK + 2 * BAND_HALF
HALO_ROWS_BF16 = 16
FFN_CHUNKS = ((0, 1024), (1024, 1024), (2048, 768))
VMEM_LIMIT = 56 * 1024 * 1024

F32 = jnp.float32
BF16 = jnp.bfloat16


def _inproj_kernel(x_ref, w_ref, q1_ref, k1_ref, v1_ref, q4_ref, k4_ref, v4_ref,
                   q16_ref, k16_ref, v16_ref, h_ref, gb_ref, nat_scr, d4_scr, *, tile):
    xb = x_ref[...].astype(BF16)

    def proj(c):
        return jnp.dot(xb, w_ref[:, c * 512:(c + 1) * 512], preferred_element_type=F32)

    def emit(val, o1_ref, o4_ref, o16_ref):
        o1_ref[...] = val.astype(BF16)
        q4, q16 = tile // 4, tile // 16
        for s in range(SLABS):
            lanes = slice(s * LANES, (s + 1) * LANES)
            nat_scr[s] = val[:, lanes]
            for r1 in range(4):
                plane = nat_scr[s, pl.ds(r1, q4, stride=4), :]
                d4_scr[s, r1 * q4:(r1 + 1) * q4, :] = plane
                o4_ref[0, r1, :, lanes] = plane.astype(BF16)
            for r1 in range(4):
                for r2 in range(4):
                    plane = d4_scr[s, pl.ds(r1 * q4 + r2, q16, stride=4), :]
                    o16_ref[0, 4 * r2 + r1, :, lanes] = plane.astype(BF16)

    emit(proj(0) * (LOG2E / math.sqrt(HEAD_DIM)), q1_ref, q4_ref, q16_ref)
    emit(proj(1), k1_ref, k4_ref, k16_ref)
    emit(proj(2), v1_ref, v4_ref, v16_ref)
    u = proj(3)
    gb_ref[...] = proj(4).astype(BF16)
    h_ref[...] = (proj(5) * u).astype(BF16)


def _inproj(x2d, w_in_bf16, batch, seq, tile):
    nt = x2d.shape[0]
    tps = seq // tile
    nat = jax.ShapeDtypeStruct((nt, 512), BF16)
    row_spec = pl.BlockSpec((tile, 512), lambda i: (i, 0))

    def plane_out(d):
        shape = jax.ShapeDtypeStruct((batch, d, seq // d, ATTN_WIDTH), BF16)
        spec = pl.BlockSpec((1, d, tile // d, ATTN_WIDTH), lambda i: (i // tps, 0, i % tps, 0))
        return shape, spec

    s4, b4 = plane_out(4)
    s16, b16 = plane_out(16)
    return pl.pallas_call(
        functools.partial(_inproj_kernel, tile=tile),
        grid=(nt // tile,),
        in_specs=[pl.BlockSpec((tile, D_MODEL), lambda i: (i, 0)),
                  pl.BlockSpec((D_MODEL, 6 * 512), lambda i: (0, 0))],
        out_specs=[row_spec] * 3 + [b4] * 3 + [b16] * 3 + [row_spec] * 2,
        out_shape=[nat] * 3 + [s4] * 3 + [s16] * 3 + [nat] * 2,
        scratch_shapes=[pltpu.VMEM((SLABS, tile, LANES), F32)] * 2,
        compiler_params=pltpu.CompilerParams(dimension_semantics=("arbitrary",),
                                             vmem_limit_bytes=VMEM_LIMIT),
        name="inproj",
    )(x2d, w_in_bf16)


def _bias_tables(dilation):
    slopes = np.array([2.0 ** (-8.0 * (h + 1) / N_HEADS) for h in range(N_HEADS)], np.float64)
    i = np.arange(Q_BLOCK)[:, None]
    j = np.arange(K_BLOCK)[None, :]
    rel = np.abs(i + BAND_HALF - j)
    band = rel <= BAND_HALF
    dist = (rel * dilation).astype(np.float64)
    bias = -slopes[:, None, None] * dist[None] * LOG2E
    tabs = []
    for valid_cols in (j >= BAND_HALF, j >= 0, j < Q_BLOCK + BAND_HALF):
        tabs.append(np.where((band & valid_cols)[None], bias, NEG))
    return np.stack(tabs).astype(np.float32)


def _attn_kernel(q_ref, kp_ref, km_ref, kn_ref, vp_ref, vm_ref, vn_ref, bias_ref,
                 o_ref, lse_ref, *, tq, n_tiles):
    nb = tq // Q_BLOCK
    tile = pl.program_id(2)
    lane = lax.broadcasted_iota(jnp.int32, (1, LANES), 1)
    low = lane < HEAD_DIM

    def chunk(prev_ref, main_ref, next_ref, c):
        if c == 0:
            return jnp.concatenate([prev_ref[0, 0], main_ref[0, 0, 0:BAND_HALF, :]], axis=0)
        if c == nb:
            return jnp.concatenate([main_ref[0, 0, tq - BAND_HALF:tq, :], next_ref[0, 0]], axis=0)
        return main_ref[0, 0, c * Q_BLOCK - BAND_HALF:c * Q_BLOCK + BAND_HALF, :]

    kt, vc = [], []
    for c in range(nb + 1):
        kc = chunk(kp_ref, km_ref, kn_ref, c)
        kt.append([kc[:, g * LANES:(g + 1) * LANES].T for g in range(SLABS)])
        vc.append(chunk(vp_ref, vm_ref, vn_ref, c))
    ones = jnp.ones((K_BLOCK, LANES), BF16)

    for b in range(nb):
        rows = slice(b * Q_BLOCK, (b + 1) * Q_BLOCK)
        if b == 0:
            variant = jnp.where(tile == 0, 0, 1)
        elif b == nb - 1:
            variant = jnp.where(tile == n_tiles - 1, 2, 1)
        else:
            variant = 1
        m_tile = jnp.zeros((Q_BLOCK, LANES), F32)
        l_tile = jnp.ones((Q_BLOCK, LANES), F32)
        for g in range(SLABS):
            cols = slice(g * LANES, (g + 1) * LANES)
            qp = q_ref[0, 0, rows, cols]
            ktp = jnp.concatenate([kt[b][g], kt[b + 1][g]], axis=1)
            vpair = jnp.concatenate([vc[b][:, cols], vc[b + 1][:, cols]], axis=0)
            vaug = jnp.concatenate([vpair, ones], axis=1)
            outs = []
            for e in range(2):
                h = 2 * g + e
                qe = jnp.where(low if e == 0 else jnp.logical_not(low), qp, jnp.zeros_like(qp))
                s = jnp.dot(qe, ktp, preferred_element_type=F32) + bias_ref[variant, h]
                m = jnp.max(s, axis=-1, keepdims=True)
                p = jnp.exp2(s - m)
                pv = jnp.dot(p.astype(BF16), vaug, preferred_element_type=F32)
                l = pv[:, LANES:]
                outs.append(pv[:, :LANES] / l)
                m_tile = jnp.where(lane == h, m, m_tile)
                l_tile = jnp.where(lane == h, l, l_tile)
            o_ref[0, 0, rows, cols] = jnp.where(low, outs[0], outs[1]).astype(BF16)
        lse_ref[0, 0, rows, :] = (m_tile + jnp.log2(l_tile)) * LN2


def _whole_plane_bias_tables(dilation):
    slopes = np.array([2.0 ** (-8.0 * (h + 1) / N_HEADS) for h in range(N_HEADS)], np.float64)
    i = np.arange(Q_BLOCK)[:, None]
    j = np.arange(2 * Q_BLOCK)[None, :]
    tabs = []
    for b in range(2):
        rel = np.abs(b * Q_BLOCK + i - j)
        bias = -slopes[:, None, None] * (rel * dilation).astype(np.float64)[None] * LOG2E
        tabs.append(np.where((rel <= BAND_HALF)[None], bias, NEG))
    return np.stack(tabs).astype(np.float32)


def _whole_plane_attn_kernel(q_ref, k_ref, v_ref, bias_ref, o_ref, lse_ref, *, planes):
    lane = lax.broadcasted_iota(jnp.int32, (1, LANES), 1)
    low = lane < HEAD_DIM
    ones = jnp.ones((2 * Q_BLOCK, LANES), BF16)
    for r in range(planes):
        m_tiles = [jnp.zeros((Q_BLOCK, LANES), F32)] * 2
        l_tiles = [jnp.ones((Q_BLOCK, LANES), F32)] * 2
        for g in range(SLABS):
            cols = slice(g * LANES, (g + 1) * LANES)
            ktp = jnp.concatenate([k_ref[0, r, 0:Q_BLOCK, cols].T, k_ref[0, r, Q_BLOCK:2 * Q_BLOCK, cols].T],
                                  axis=1)
            vaug = jnp.concatenate([v_ref[0, r, :, cols], ones], axis=1)
            for b in range(2):
                rows = slice(b * Q_BLOCK, (b + 1) * Q_BLOCK)
                qp = q_ref[0, r, rows, cols]
                outs = []
                for e in range(2):
                    h = 2 * g + e
                    qe = jnp.where(low if e == 0 else jnp.logical_not(low), qp, jnp.zeros_like(qp))
                    s = jnp.dot(qe, ktp, preferred_element_type=F32) + bias_ref[b, h]
                    m = jnp.max(s, axis=-1, keepdims=True)
                    p = jnp.exp2(s - m)
                    pv = jnp.dot(p.astype(BF16), vaug, preferred_element_type=F32)
                    l = pv[:, LANES:]
                    outs.append(pv[:, :LANES] / l)
                    m_tiles[b] = jnp.where(lane == h, m, m_tiles[b])
                    l_tiles[b] = jnp.where(lane == h, l, l_tiles[b])
                o_ref[0, r, rows, cols] = jnp.where(low, outs[0], outs[1]).astype(BF16)
        for b in range(2):
            lse_ref[0, r, b * Q_BLOCK:(b + 1) * Q_BLOCK, :] = (m_tiles[b] + jnp.log2(l_tiles[b])) * LN2


def _whole_plane_attention(q, k, v, planes):
    batch, d, n, _ = q.shape
    blk = lambda width: pl.BlockSpec((1, planes, n, width), lambda b, r: (b, r, 0, 0))
    return pl.pallas_call(
        functools.partial(_whole_plane_attn_kernel, planes=planes),
        grid=(batch, d // planes),
        in_specs=[blk(ATTN_WIDTH)] * 3
                 + [pl.BlockSpec((2, N_HEADS, Q_BLOCK, 2 * Q_BLOCK), lambda b, r: (0, 0, 0, 0))],
        out_specs=[blk(ATTN_WIDTH), blk(LANES)],
        out_shape=[jax.ShapeDtypeStruct((batch, d, n, ATTN_WIDTH), BF16),
                   jax.ShapeDtypeStruct((batch, d, n, LANES), F32)],
        compiler_params=pltpu.CompilerParams(dimension_semantics=("arbitrary",) * 2,
                                             vmem_limit_bytes=VMEM_LIMIT),
        name=f"band_attn_whole_d{d}",
    )(q, k, v, jnp.asarray(_whole_plane_bias_tables(d)))


def _band_attention(q, k, v, dilation):
    batch, d, n, _ = q.shape
    assert d == dilation and n % Q_BLOCK == 0 and n >= 2 * Q_BLOCK
    if n == 2 * Q_BLOCK:
        return _whole_plane_attention(q, k, v, planes=math.gcd(d, 4))
    tq = min(n, 1024)
    halo_per_tile = tq // BAND_HALF
    n_halo_blocks = n // BAND_HALF
    bias = jnp.asarray(_bias_tables(d))

    main = pl.BlockSpec((1, 1, tq, ATTN_WIDTH), lambda b, r, i: (b, r, i, 0))
    prev = pl.BlockSpec((1, 1, BAND_HALF, ATTN_WIDTH),
                        lambda b, r, i: (b, r, jnp.maximum(i * halo_per_tile - 1, 0), 0))
    nxt = pl.BlockSpec((1, 1, BAND_HALF, ATTN_WIDTH),
                       lambda b, r, i: (b, r, jnp.minimum((i + 1) * halo_per_tile, n_halo_blocks - 1), 0))
    return pl.pallas_call(
        functools.partial(_attn_kernel, tq=tq, n_tiles=n // tq),
        grid=(batch, d, n // tq),
        in_specs=[main, prev, main, nxt, prev, main, nxt,
                  pl.BlockSpec((3, N_HEADS, Q_BLOCK, K_BLOCK), lambda b, r, i: (0, 0, 0, 0))],
        out_specs=[main, pl.BlockSpec((1, 1, tq, LANES), lambda b, r, i: (b, r, i, 0))],
        out_shape=[jax.ShapeDtypeStruct((batch, d, n, ATTN_WIDTH), BF16),
                   jax.ShapeDtypeStruct((batch, d, n, LANES), F32)],
        compiler_params=pltpu.CompilerParams(dimension_semantics=("arbitrary",) * 3,
                                             vmem_limit_bytes=VMEM_LIMIT),
        name=f"band_attn_d{d}",
    )(q, k, k, k, v, v, v, bias)


def _layernorm(x, g, b):
    mu = jnp.mean(x, axis=-1, keepdims=True)
    xc = x - mu
    var = jnp.mean(xc * xc, axis=-1, keepdims=True)
    return xc * lax.rsqrt(var + LN_EPS) * g + b


def _rmsnorm(x, g):
    return x * lax.rsqrt(jnp.mean(x * x, axis=-1, keepdims=True) + LN_EPS) * g


def _post_kernel(x_ref, o1_ref, o4_ref, o16_ref, l1_ref, l4_ref, l16_ref,
                 h_ref, hprev_ref, hnext_ref, gb_ref, convw_ref, gattn_ref, gconv_ref, wo_ref,
                 ln1g_ref, ln1b_ref, wg_ref, wu_ref, wd_ref, ln2g_ref, ln2b_ref, out_ref,
                 mix_scr, o4_scr, o16_scr, l4_scr, l16_scr, *, tile, tiles_per_seq, n_tiles):
    j = pl.program_id(0)

    @pl.when(j == 0)
    def _():
        mix_scr[...] = jnp.zeros_like(mix_scr)

    mix = jnp.dot(mix_scr[...], wo_ref[...], preferred_element_type=F32)
    x1 = _layernorm(ALPHA * x_ref[...] + mix, ln1g_ref[...], ln1b_ref[...])
    x1b = x1.astype(BF16)
    ffn = None
    for c0, cw_ in FFN_CHUNKS:
        gate = jnp.dot(x1b, wg_ref[:, c0:c0 + cw_], preferred_element_type=F32)
        up = jnp.dot(x1b, wu_ref[:, c0:c0 + cw_], preferred_element_type=F32)
        act = (gate * (1.0 / (1.0 + jnp.exp(-gate))) * up).astype(BF16)
        part = jnp.dot(act, wd_ref[c0:c0 + cw_, :], preferred_element_type=F32)
        ffn = part if ffn is None else ffn + part
    out_ref[...] = _layernorm(ALPHA * x1 + ffn, ln2g_ref[...], ln2b_ref[...])

    lane = lax.broadcasted_iota(jnp.int32, (1, LANES), 1)

    def interleave(src_ref, dst_scr, d, slabs):
        for r in range(d):
            for s in range(slabs):
                dst_scr[s, pl.ds(r, tile // d, stride=d), :] = \
                    src_ref[0, r, :, s * LANES:(s + 1) * LANES].astype(F32)

    interleave(o4_ref, o4_scr, 4, SLABS)
    interleave(o16_ref, o16_scr, 16, SLABS)
    interleave(l4_ref, l4_scr, 4, 1)
    interleave(l16_ref, l16_scr, 16, 1)

    lses = (l1_ref[...], l4_scr[0], l16_scr[0])
    top = jnp.maximum(jnp.maximum(lses[0], lses[1]), lses[2])
    es = [jnp.exp(l - top) for l in lses]
    inv = 1.0 / (es[0] + es[1] + es[2])
    ws = [e * inv for e in es]
    pairs = []
    for g in range(SLABS):
        cols = slice(g * LANES, (g + 1) * LANES)
        os_ = (o1_ref[:, cols].astype(F32), o4_scr[g], o16_scr[g])
        acc = None
        for w, o in zip(ws, os_):
            wpair = jnp.where(lane < HEAD_DIM, w[:, 2 * g:2 * g + 1], w[:, 2 * g + 1:2 * g + 2])
            acc = wpair * o if acc is None else acc + wpair * o
        pairs.append(acc)
    attn = jnp.concatenate(pairs, axis=-1)

    hf = h_ref[...].astype(F32)
    pos = jnp.minimum(j, n_tiles - 1) % tiles_per_seq
    prev_row = jnp.where(pos == 0, 0.0, hprev_ref[HALO_ROWS_BF16 - 1:HALO_ROWS_BF16, :].astype(F32))
    next_row = jnp.where(pos == tiles_per_seq - 1, 0.0, hnext_ref[0:1, :].astype(F32))
    row = lax.broadcasted_iota(jnp.int32, (tile, 1), 0)
    h_m1 = jnp.where(row == 0, prev_row, pltpu.roll(hf, 1, 0))
    h_p1 = jnp.where(row == tile - 1, next_row, pltpu.roll(hf, tile - 1, 0))
    cw = convw_ref[...]
    conv = gb_ref[...].astype(F32) * (cw[0:1, :] * h_m1 + cw[1:2, :] * hf + cw[2:3, :] * h_p1)

    mix_scr[:, 0:ATTN_WIDTH] = _rmsnorm(attn, gattn_ref[...]).astype(BF16)
    mix_scr[:, ATTN_WIDTH:D_MODEL] = _rmsnorm(conv, gconv_ref[...]).astype(BF16)


def _post(x2d, os_, lses, h, gb, params, seq, tile):
    nt = x2d.shape[0]
    n_tiles = nt // tile
    tps = seq // tile
    halo_per_tile = tile // HALO_ROWS_BF16
    n_halo = nt // HALO_ROWS_BF16
    cur = lambda j: jnp.minimum(j, n_tiles - 1)
    old = lambda j: jnp.maximum(j - 1, 0)
    row = lambda width: pl.BlockSpec((tile, width), lambda j: (cur(j), 0))
    planes = lambda d, width: pl.BlockSpec((1, d, tile // d, width),
                                           lambda j: (cur(j) // tps, 0, cur(j) % tps, 0))
    const = lambda shape: pl.BlockSpec(shape, lambda j: (0,) * len(shape), pipeline_mode=pl.Buffered(1))
    hprev = pl.BlockSpec((HALO_ROWS_BF16, CONV_WIDTH),
                         lambda j: (jnp.maximum(cur(j) * halo_per_tile - 1, 0), 0))
    hnext = pl.BlockSpec((HALO_ROWS_BF16, CONV_WIDTH),
                         lambda j: (jnp.minimum((cur(j) + 1) * halo_per_tile, n_halo - 1), 0))
    (conv_w, g_attn, g_conv, w_o, ln1_g, ln1_b, w_gate, w_up, w_down, ln2_g, ln2_b) = params
    vec = lambda a: a.reshape(1, -1).astype(F32)
    return pl.pallas_call(
        functools.partial(_post_kernel, tile=tile, tiles_per_seq=tps, n_tiles=n_tiles),
        grid=(n_tiles + 1,),
        in_specs=[pl.BlockSpec((tile, D_MODEL), lambda j: (old(j), 0)),
                  row(ATTN_WIDTH), planes(4, ATTN_WIDTH), planes(16, ATTN_WIDTH),
                  row(LANES), planes(4, LANES), planes(16, LANES),
                  row(CONV_WIDTH), hprev, hnext, row(CONV_WIDTH),
                  const((3, CONV_WIDTH)), const((1, ATTN_WIDTH)), const((1, CONV_WIDTH)),
                  const((D_MODEL, D_MODEL)), const((1, D_MODEL)), const((1, D_MODEL)),
                  const((D_MODEL, D_FF)), const((D_MODEL, D_FF)), const((D_FF, D_MODEL)),
                  const((1, D_MODEL)), const((1, D_MODEL))],
        out_specs=pl.BlockSpec((tile, D_MODEL), lambda j: (old(j), 0)),
        out_shape=jax.ShapeDtypeStruct((nt, D_MODEL), F32),
        scratch_shapes=[pltpu.VMEM((tile, D_MODEL), BF16),
                        pltpu.VMEM((SLABS, tile, LANES), F32), pltpu.VMEM((SLABS, tile, LANES), F32),
                        pltpu.VMEM((1, tile, LANES), F32), pltpu.VMEM((1, tile, LANES), F32)],
        compiler_params=pltpu.CompilerParams(dimension_semantics=("arbitrary",),
                                             vmem_limit_bytes=VMEM_LIMIT),
        name="post",
    )(x2d, *os_, *lses, h, h, h, gb, conv_w.astype(F32), vec(g_attn), vec(g_conv), w_o.astype(BF16),
      vec(ln1_g), vec(ln1_b), w_gate.astype(BF16), w_up.astype(BF16), w_down.astype(BF16),
      vec(ln2_g), vec(ln2_b))


def _layer(x, w_in_bf16, params):
    batch, seq, _ = x.shape
    nt = batch * seq
    x2d = x.reshape(nt, D_MODEL)
    q1, k1, v1, q4, k4, v4, q16, k16, v16, h, gb = _inproj(x2d, w_in_bf16, batch, seq, tile=1024)
    as_planes = lambda a: a.reshape(batch, 1, seq, ATTN_WIDTH)
    o1, l1 = _band_attention(as_planes(q1), as_planes(k1), as_planes(v1), 1)
    o4, l4 = _band_attention(q4, k4, v4, 4)
    o16, l16 = _band_attention(q16, k16, v16, 16)
    y = _post(x2d, (o1.reshape(nt, ATTN_WIDTH), o4, o16), (l1.reshape(nt, LANES), l4, l16),
              h, gb, params, seq, tile=512)
    return y.reshape(batch, seq, D_MODEL)


def kernel(x_prompt, x_sample, w_in, conv_w, g_attn, g_conv, w_o, ln1_g, ln1_b, w_gate, w_up, w_down, ln2_g, ln2_b):
    w_in_bf16 = w_in.astype(BF16)
    params = (conv_w, g_attn, g_conv, w_o, ln1_g, ln1_b, w_gate, w_up, w_down, ln2_g, ln2_b)
    y_prompt = x_prompt
    y_sample = x_sample
    for _ in range(DEPTH):
        y_prompt = _layer(y_prompt, w_in_bf16, params)
        y_sample = _layer(y_sample, w_in_bf16, params)
    return (y_prompt, y_sample)
```

```python
import functools
import math

import numpy as np
import jax
import jax.numpy as jnp
from jax import lax
from jax.experimental import pallas as pl
from jax.experimental.pallas import tpu as pltpu

D_MODEL = 1024
ATTN_WIDTH = 512
CONV_WIDTH = 512
HEAD_DIM = 64
N_HEADS = 8
ATTN_CONFIGS = ((128, 1), (512, 4), (2048, 16))
BAND_HALF = 64
D_FF = 2816
DEPTH = 1
ALPHA = (2.0 * DEPTH) ** 0.25
LN_EPS = 1e-5
NEG = -1e30
LOG2E = math.log2(math.e)
LN2 = math.log(2.0)

LANES = 128
SLABS = ATTN_WIDTH // LANES
Q_BLOCK = 128
K_BLOCK = Q_BLOCK + 2 * BAND_HALF
HALO_ROWS_BF16 = 16
FFN_CHUNKS = ((0, 1024), (1024, 1024), (2048, 768))
VMEM_LIMIT = 56 * 1024 * 1024

F32 = jnp.float32
BF16 = jnp.bfloat16


def _inproj_kernel(x_ref, w_ref, qkv1_ref, qkv4_ref, qkv16_ref, h_ref, gb_ref, nat_scr, d4_scr, *, tile):
    xb = x_ref[...].astype(BF16)

    def proj(c):
        return jnp.dot(xb, w_ref[:, c * 512:(c + 1) * 512], preferred_element_type=F32)

    def emit(val, c):
        qkv1_ref[:, c * ATTN_WIDTH:(c + 1) * ATTN_WIDTH] = val.astype(BF16)
        q4, q16 = tile // 4, tile // 16
        for s in range(SLABS):
            lanes = slice(s * LANES, (s + 1) * LANES)
            out_lanes = slice(c * ATTN_WIDTH + s * LANES, c * ATTN_WIDTH + (s + 1) * LANES)
            nat_scr[s] = val[:, lanes]
            for r1 in range(4):
                plane = nat_scr[s, pl.ds(r1, q4, stride=4), :]
                d4_scr[s, r1 * q4:(r1 + 1) * q4, :] = plane
                qkv4_ref[0, r1, :, out_lanes] = plane.astype(BF16)
            for r1 in range(4):
                for r2 in range(4):
                    plane = d4_scr[s, pl.ds(r1 * q4 + r2, q16, stride=4), :]
                    qkv16_ref[0, 4 * r2 + r1, :, out_lanes] = plane.astype(BF16)

    emit(proj(0) * (LOG2E / math.sqrt(HEAD_DIM)), 0)
    emit(proj(1), 1)
    emit(proj(2), 2)
    u = proj(3)
    gb_ref[...] = proj(4).astype(BF16)
    h_ref[...] = (proj(5) * u).astype(BF16)


def _inproj(x2d, w_in_bf16, batch, seq, tile):
    nt = x2d.shape[0]
    tps = seq // tile
    nat = jax.ShapeDtypeStruct((nt, 512), BF16)
    row_spec = pl.BlockSpec((tile, 512), lambda i: (i, 0))

    def plane_out(d):
        shape = jax.ShapeDtypeStruct((batch, d, seq // d, 3 * ATTN_WIDTH), BF16)
        spec = pl.BlockSpec((1, d, tile // d, 3 * ATTN_WIDTH), lambda i: (i // tps, 0, i % tps, 0))
        return shape, spec

    s4, b4 = plane_out(4)
    s16, b16 = plane_out(16)
    nat3 = jax.ShapeDtypeStruct((nt, 3 * ATTN_WIDTH), BF16)
    row3_spec = pl.BlockSpec((tile, 3 * ATTN_WIDTH), lambda i: (i, 0))
    return pl.pallas_call(
        functools.partial(_inproj_kernel, tile=tile),
        grid=(nt // tile,),
        in_specs=[pl.BlockSpec((tile, D_MODEL), lambda i: (i, 0)),
                  pl.BlockSpec((D_MODEL, 6 * 512), lambda i: (0, 0))],
        out_specs=[row3_spec, b4, b16] + [row_spec] * 2,
        out_shape=[nat3, s4, s16] + [nat] * 2,
        scratch_shapes=[pltpu.VMEM((SLABS, tile, LANES), F32)] * 2,
        compiler_params=pltpu.CompilerParams(dimension_semantics=("arbitrary",),
                                             vmem_limit_bytes=VMEM_LIMIT),
        name="inproj",
    )(x2d, w_in_bf16)


def _bias_tables(dilation):
    slopes = np.array([2.0 ** (-8.0 * (h + 1) / N_HEADS) for h in range(N_HEADS)], np.float64)
    i = np.arange(Q_BLOCK)[:, None]
    j = np.arange(K_BLOCK)[None, :]
    rel = np.abs(i + BAND_HALF - j)
    band = rel <= BAND_HALF
    dist = (rel * dilation).astype(np.float64)
    bias = -slopes[:, None, None] * dist[None] * LOG2E
    tabs = []
    for valid_cols in (j >= BAND_HALF, j >= 0, j < Q_BLOCK + BAND_HALF):
        tabs.append(np.where((band & valid_cols)[None], bias, NEG))
    return np.stack(tabs).astype(np.float32)


def _attn_kernel(q_ref, kp_ref, km_ref, kn_ref, vp_ref, vm_ref, vn_ref, bias_ref,
                 o_ref, lse_ref, *, tq, n_tiles):
    nb = tq // Q_BLOCK
    tile = pl.program_id(2)
    lane = lax.broadcasted_iota(jnp.int32, (1, LANES), 1)
    low = lane < HEAD_DIM

    def chunk(prev_ref, main_ref, next_ref, c):
        if c == 0:
            return jnp.concatenate([prev_ref[0, 0], main_ref[0, 0, 0:BAND_HALF, :]], axis=0)
        if c == nb:
            return jnp.concatenate([main_ref[0, 0, tq - BAND_HALF:tq, :], next_ref[0, 0]], axis=0)
        return main_ref[0, 0, c * Q_BLOCK - BAND_HALF:c * Q_BLOCK + BAND_HALF, :]

    kt, vc = [], []
    for c in range(nb + 1):
        kc = chunk(kp_ref, km_ref, kn_ref, c)
        kt.append([kc[:, g * LANES:(g + 1) * LANES].T for g in range(SLABS)])
        vc.append(chunk(vp_ref, vm_ref, vn_ref, c))
    ones = jnp.ones((K_BLOCK, LANES), BF16)

    for b in range(nb):
        rows = slice(b * Q_BLOCK, (b + 1) * Q_BLOCK)
        if b == 0:
            variant = jnp.where(tile == 0, 0, 1)
        elif b == nb - 1:
            variant = jnp.where(tile == n_tiles - 1, 2, 1)
        else:
            variant = 1
        m_tile = jnp.zeros((Q_BLOCK, LANES), F32)
        l_tile = jnp.ones((Q_BLOCK, LANES), F32)
        for g in range(SLABS):
            cols = slice(g * LANES, (g + 1) * LANES)
            qp = q_ref[0, 0, rows, cols]
            ktp = jnp.concatenate([kt[b][g], kt[b + 1][g]], axis=1)
            vpair = jnp.concatenate([vc[b][:, cols], vc[b + 1][:, cols]], axis=0)
            vaug = jnp.concatenate([vpair, ones], axis=1)
            outs = []
            for e in range(2):
                h = 2 * g + e
                qe = jnp.where(low if e == 0 else jnp.logical_not(low), qp, jnp.zeros_like(qp))
                s = jnp.dot(qe, ktp, preferred_element_type=F32) + bias_ref[variant, h]
                m = jnp.max(s, axis=-1, keepdims=True)
                p = jnp.exp2(s - m)
                pv = jnp.dot(p.astype(BF16), vaug, preferred_element_type=F32)
                l = pv[:, LANES:]
                outs.append(pv[:, :LANES] / l)
                m_tile = jnp.where(lane == h, m, m_tile)
                l_tile = jnp.where(lane == h, l, l_tile)
            o_ref[0, 0, rows, cols] = jnp.where(low, outs[0], outs[1]).astype(BF16)
        lse_ref[0, 0, rows, :] = (m_tile + jnp.log2(l_tile)) * LN2


def _whole_plane_bias_tables(dilation):
    slopes = np.array([2.0 ** (-8.0 * (h + 1) / N_HEADS) for h in range(N_HEADS)], np.float64)
    i = np.arange(Q_BLOCK)[:, None]
    j = np.arange(2 * Q_BLOCK)[None, :]
    tabs = []
    for b in range(2):
        rel = np.abs(b * Q_BLOCK + i - j)
        bias = -slopes[:, None, None] * (rel * dilation).astype(np.float64)[None] * LOG2E
        tabs.append(np.where((rel <= BAND_HALF)[None], bias, NEG))
    return np.stack(tabs).astype(np.float32)


def _whole_plane_attn_kernel(q_ref, k_ref, v_ref, bias_ref, o_ref, lse_ref, *, planes):
    lane = lax.broadcasted_iota(jnp.int32, (1, LANES), 1)
    low = lane < HEAD_DIM
    ones = jnp.ones((2 * Q_BLOCK, LANES), BF16)
    for r in range(planes):
        m_tiles = [jnp.zeros((Q_BLOCK, LANES), F32)] * 2
        l_tiles = [jnp.ones((Q_BLOCK, LANES), F32)] * 2
        for g in range(SLABS):
            cols = slice(g * LANES, (g + 1) * LANES)
            ktp = jnp.concatenate([k_ref[0, r, 0:Q_BLOCK, cols].T, k_ref[0, r, Q_BLOCK:2 * Q_BLOCK, cols].T],
                                  axis=1)
            vaug = jnp.concatenate([v_ref[0, r, :, cols], ones], axis=1)
            for b in range(2):
                rows = slice(b * Q_BLOCK, (b + 1) * Q_BLOCK)
                qp = q_ref[0, r, rows, cols]
                outs = []
                for e in range(2):
                    h = 2 * g + e
                    qe = jnp.where(low if e == 0 else jnp.logical_not(low), qp, jnp.zeros_like(qp))
                    s = jnp.dot(qe, ktp, preferred_element_type=F32) + bias_ref[b, h]
                    m = jnp.max(s, axis=-1, keepdims=True)
                    p = jnp.exp2(s - m)
                    pv = jnp.dot(p.astype(BF16), vaug, preferred_element_type=F32)
                    l = pv[:, LANES:]
                    outs.append(pv[:, :LANES] / l)
                    m_tiles[b] = jnp.where(lane == h, m, m_tiles[b])
                    l_tiles[b] = jnp.where(lane == h, l, l_tiles[b])
                o_ref[0, r, rows, cols] = jnp.where(low, outs[0], outs[1]).astype(BF16)
        for b in range(2):
            lse_ref[0, r, b * Q_BLOCK:(b + 1) * Q_BLOCK, :] = (m_tiles[b] + jnp.log2(l_tiles[b])) * LN2


def _whole_plane_attention(qkv, planes):
    batch, d, n, _ = qkv.shape
    blk = lambda width, c=0: pl.BlockSpec((1, planes, n, width), lambda b, r: (b, r, 0, c))
    return pl.pallas_call(
        functools.partial(_whole_plane_attn_kernel, planes=planes),
        grid=(batch, d // planes),
        in_specs=[blk(ATTN_WIDTH, c) for c in range(3)]
                 + [pl.BlockSpec((2, N_HEADS, Q_BLOCK, 2 * Q_BLOCK), lambda b, r: (0, 0, 0, 0))],
        out_specs=[blk(ATTN_WIDTH), blk(LANES)],
        out_shape=[jax.ShapeDtypeStruct((batch, d, n, ATTN_WIDTH), BF16),
                   jax.ShapeDtypeStruct((batch, d, n, LANES), F32)],
        compiler_params=pltpu.CompilerParams(dimension_semantics=("arbitrary",) * 2,
                                             vmem_limit_bytes=VMEM_LIMIT),
        name=f"band_attn_whole_d{d}",
    )(qkv, qkv, qkv, jnp.asarray(_whole_plane_bias_tables(d)))


def _band_attention(qkv, dilation):
    batch, d, n, _ = qkv.shape
    assert d == dilation and n % Q_BLOCK == 0 and n >= 2 * Q_BLOCK
    if n == 2 * Q_BLOCK:
        return _whole_plane_attention(qkv, planes=math.gcd(d, 4))
    tq = min(n, 1024)
    halo_per_tile = tq // BAND_HALF
    n_halo_blocks = n // BAND_HALF
    bias = jnp.asarray(_bias_tables(d))

    main = lambda c: pl.BlockSpec((1, 1, tq, ATTN_WIDTH), lambda b, r, i: (b, r, i, c))
    prev = lambda c: pl.BlockSpec((1, 1, BAND_HALF, ATTN_WIDTH),
                                  lambda b, r, i: (b, r, jnp.maximum(i * halo_per_tile - 1, 0), c))
    nxt = lambda c: pl.BlockSpec(
        (1, 1, BAND_HALF, ATTN_WIDTH),
        lambda b, r, i: (b, r, jnp.minimum((i + 1) * halo_per_tile, n_halo_blocks - 1), c))
    return pl.pallas_call(
        functools.partial(_attn_kernel, tq=tq, n_tiles=n // tq),
        grid=(batch, d, n // tq),
        in_specs=[main(0), prev(1), main(1), nxt(1), prev(2), main(2), nxt(2),
                  pl.BlockSpec((3, N_HEADS, Q_BLOCK, K_BLOCK), lambda b, r, i: (0, 0, 0, 0))],
        out_specs=[main(0), pl.BlockSpec((1, 1, tq, LANES), lambda b, r, i: (b, r, i, 0))],
        out_shape=[jax.ShapeDtypeStruct((batch, d, n, ATTN_WIDTH), BF16),
                   jax.ShapeDtypeStruct((batch, d, n, LANES), F32)],
        compiler_params=pltpu.CompilerParams(dimension_semantics=("arbitrary",) * 3,
                                             vmem_limit_bytes=VMEM_LIMIT),
        name=f"band_attn_d{d}",
    )(*([qkv] * 7), bias)


def _layernorm(x, g, b):
    mu = jnp.mean(x, axis=-1, keepdims=True)
    xc = x - mu
    var = jnp.mean(xc * xc, axis=-1, keepdims=True)
    return xc * lax.rsqrt(var + LN_EPS) * g + b


def _rmsnorm(x, g):
    return x * lax.rsqrt(jnp.mean(x * x, axis=-1, keepdims=True) + LN_EPS) * g


def _post_kernel(x_ref, o1_ref, o4_ref, o16_ref, l1_ref, l4_ref, l16_ref,
                 h_ref, hprev_ref, hnext_ref, gb_ref, convw_ref, gattn_ref, gconv_ref, wo_ref,
                 ln1g_ref, ln1b_ref, wg_ref, wu_ref, wd_ref, ln2g_ref, ln2b_ref, out_ref,
                 mix_scr, o4_scr, o16_scr, l4_scr, l16_scr, *, tile, tiles_per_seq, n_tiles):
    j = pl.program_id(0)

    @pl.when(j == 0)
    def _():
        mix_scr[...] = jnp.zeros_like(mix_scr)

    mix = jnp.dot(mix_scr[...], wo_ref[...], preferred_element_type=F32)
    x1 = _layernorm(ALPHA * x_ref[...] + mix, ln1g_ref[...], ln1b_ref[...])
    x1b = x1.astype(BF16)
    ffn = None
    for c0, cw_ in FFN_CHUNKS:
        gate = jnp.dot(x1b, wg_ref[:, c0:c0 + cw_], preferred_element_type=F32)
        up = jnp.dot(x1b, wu_ref[:, c0:c0 + cw_], preferred_element_type=F32)
        act = (gate * (1.0 / (1.0 + jnp.exp(-gate))) * up).astype(BF16)
        part = jnp.dot(act, wd_ref[c0:c0 + cw_, :], preferred_element_type=F32)
        ffn = part if ffn is None else ffn + part
    out_ref[...] = _layernorm(ALPHA * x1 + ffn, ln2g_ref[...], ln2b_ref[...])

    lane = lax.broadcasted_iota(jnp.int32, (1, LANES), 1)

    def interleave(src_ref, dst_scr, d, slabs):
        for r in range(d):
            for s in range(slabs):
                dst_scr[s, pl.ds(r, tile // d, stride=d), :] = \
                    src_ref[0, r, :, s * LANES:(s + 1) * LANES].astype(F32)

    interleave(o4_ref, o4_scr, 4, SLABS)
    interleave(o16_ref, o16_scr, 16, SLABS)
    interleave(l4_ref, l4_scr, 4, 1)
    interleave(l16_ref, l16_scr, 16, 1)

    lses = (l1_ref[...], l4_scr[0], l16_scr[0])
    top = jnp.maximum(jnp.maximum(lses[0], lses[1]), lses[2])
    es = [jnp.exp(l - top) for l in lses]
    inv = 1.0 / (es[0] + es[1] + es[2])
    ws = [e * inv for e in es]
    pairs = []
    for g in range(SLABS):
        cols = slice(g * LANES, (g + 1) * LANES)
        os_ = (o1_ref[:, cols].astype(F32), o4_scr[g], o16_scr[g])
        acc = None
        for w, o in zip(ws, os_):
            wpair = jnp.where(lane < HEAD_DIM, w[:, 2 * g:2 * g + 1], w[:, 2 * g + 1:2 * g + 2])
            acc = wpair * o if acc is None else acc + wpair * o
        pairs.append(acc)
    attn = jnp.concatenate(pairs, axis=-1)

    hf = h_ref[...].astype(F32)
    pos = jnp.minimum(j, n_tiles - 1) % tiles_per_seq
    prev_row = jnp.where(pos == 0, 0.0, hprev_ref[HALO_ROWS_BF16 - 1:HALO_ROWS_BF16, :].astype(F32))
    next_row = jnp.where(pos == tiles_per_seq - 1, 0.0, hnext_ref[0:1, :].astype(F32))
    row = lax.broadcasted_iota(jnp.int32, (tile, 1), 0)
    h_m1 = jnp.where(row == 0, prev_row, pltpu.roll(hf, 1, 0))
    h_p1 = jnp.where(row == tile - 1, next_row, pltpu.roll(hf, tile - 1, 0))
    cw = convw_ref[...]
    conv = gb_ref[...].astype(F32) * (cw[0:1, :] * h_m1 + cw[1:2, :] * hf + cw[2:3, :] * h_p1)

    mix_scr[:, 0:ATTN_WIDTH] = _rmsnorm(attn, gattn_ref[...]).astype(BF16)
    mix_scr[:, ATTN_WIDTH:D_MODEL] = _rmsnorm(conv, gconv_ref[...]).astype(BF16)


def _post(x2d, os_, lses, h, gb, params, seq, tile):
    nt = x2d.shape[0]
    n_tiles = nt // tile
    tps = seq // tile
    halo_per_tile = tile // HALO_ROWS_BF16
    n_halo = nt // HALO_ROWS_BF16
    cur = lambda j: jnp.minimum(j, n_tiles - 1)
    old = lambda j: jnp.maximum(j - 1, 0)
    row = lambda width: pl.BlockSpec((tile, width), lambda j: (cur(j), 0))
    planes = lambda d, width: pl.BlockSpec((1, d, tile // d, width),
                                           lambda j: (cur(j) // tps, 0, cur(j) % tps, 0))
    const = lambda shape: pl.BlockSpec(shape, lambda j: (0,) * len(shape), pipeline_mode=pl.Buffered(1))
    hprev = pl.BlockSpec((HALO_ROWS_BF16, CONV_WIDTH),
                         lambda j: (jnp.maximum(cur(j) * halo_per_tile - 1, 0), 0))
    hnext = pl.BlockSpec((HALO_ROWS_BF16, CONV_WIDTH),
                         lambda j: (jnp.minimum((cur(j) + 1) * halo_per_tile, n_halo - 1), 0))
    (conv_w, g_attn, g_conv, w_o, ln1_g, ln1_b, w_gate, w_up, w_down, ln2_g, ln2_b) = params
    vec = lambda a: a.reshape(1, -1).astype(F32)
    return pl.pallas_call(
        functools.partial(_post_kernel, tile=tile, tiles_per_seq=tps, n_tiles=n_tiles),
        grid=(n_tiles + 1,),
        in_specs=[pl.BlockSpec((tile, D_MODEL), lambda j: (old(j), 0)),
                  row(ATTN_WIDTH), planes(4, ATTN_WIDTH), planes(16, ATTN_WIDTH),
                  row(LANES), planes(4, LANES), planes(16, LANES),
                  row(CONV_WIDTH), hprev, hnext, row(CONV_WIDTH),
                  const((3, CONV_WIDTH)), const((1, ATTN_WIDTH)), const((1, CONV_WIDTH)),
                  const((D_MODEL, D_MODEL)), const((1, D_MODEL)), const((1, D_MODEL)),
                  const((D_MODEL, D_FF)), const((D_MODEL, D_FF)), const((D_FF, D_MODEL)),
                  const((1, D_MODEL)), const((1, D_MODEL))],
        out_specs=pl.BlockSpec((tile, D_MODEL), lambda j: (old(j), 0)),
        out_shape=jax.ShapeDtypeStruct((nt, D_MODEL), F32),
        scratch_shapes=[pltpu.VMEM((tile, D_MODEL), BF16),
                        pltpu.VMEM((SLABS, tile, LANES), F32), pltpu.VMEM((SLABS, tile, LANES), F32),
                        pltpu.VMEM((1, tile, LANES), F32), pltpu.VMEM((1, tile, LANES), F32)],
        compiler_params=pltpu.CompilerParams(dimension_semantics=("arbitrary",),
                                             vmem_limit_bytes=VMEM_LIMIT),
        name="post",
    )(x2d, *os_, *lses, h, h, h, gb, conv_w.astype(F32), vec(g_attn), vec(g_conv), w_o.astype(BF16),
      vec(ln1_g), vec(ln1_b), w_gate.astype(BF16), w_up.astype(BF16), w_down.astype(BF16),
      vec(ln2_g), vec(ln2_b))


def _layer(x, w_in_bf16, params):
    batch, seq, _ = x.shape
    nt = batch * seq
    x2d = x.reshape(nt, D_MODEL)
    qkv1, qkv4, qkv16, h, gb = _inproj(x2d, w_in_bf16, batch, seq, tile=1024)
    o1, l1 = _band_attention(qkv1.reshape(batch, 1, seq, 3 * ATTN_WIDTH), 1)
    o4, l4 = _band_attention(qkv4, 4)
    o16, l16 = _band_attention(qkv16, 16)
    y = _post(x2d, (o1.reshape(nt, ATTN_WIDTH), o4, o16), (l1.reshape(nt, LANES), l4, l16),
              h, gb, params, seq, tile=512)
    return y.reshape(batch, seq, D_MODEL)


def kernel(x_prompt, x_sample, w_in, conv_w, g_attn, g_conv, w_o, ln1_g, ln1_b, w_gate, w_up, w_down, ln2_g, ln2_b):
    w_in_bf16 = w_in.astype(BF16)
    params = (conv_w, g_attn, g_conv, w_o, ln1_g, ln1_b, w_gate, w_up, w_down, ln2_g, ln2_b)
    y_prompt = x_prompt
    y_sample = x_sample
    for _ in range(DEPTH):
        y_prompt = _layer(y_prompt, w_in_bf16, params)
        y_sample = _layer(y_sample, w_in_bf16, params)
    return (y_prompt, y_sample)
```

```python
import functools
import math

import numpy as np
import jax
import jax.numpy as jnp
from jax import lax
from jax.experimental import pallas as pl
from jax.experimental.pallas import tpu as pltpu

D_MODEL = 1024
ATTN_WIDTH = 512
CONV_WIDTH = 512
HEAD_DIM = 64
N_HEADS = 8
ATTN_CONFIGS = ((128, 1), (512, 4), (2048, 16))
BAND_HALF = 64
D_FF = 2816
DEPTH = 1
ALPHA = (2.0 * DEPTH) ** 0.25
LN_EPS = 1e-5
NEG = -1e30
LOG2E = math.log2(math.e)
LN2 = math.log(2.0)

LANES = 128
SLABS = ATTN_WIDTH // LANES
Q_BLOCK = 128
K_BLOCK = Q_BLOCK + 2 * BAND_HALF
MAX_ATTN_TILE = 2048
HALO_ROWS_BF16 = 16
FFN_CHUNKS = ((0, 1024), (1024, 1024), (2048, 768))
VMEM_LIMIT = 56 * 1024 * 1024

F32 = jnp.float32
BF16 = jnp.bfloat16


def _inproj_kernel(x_ref, w_ref, qkv1_ref, qkv4_ref, qkv16_ref, h_ref, gb_ref, nat_scr, d4_scr, *, tile):
    xb = x_ref[...].astype(BF16)

    def proj(c):
        return jnp.dot(xb, w_ref[:, c * 512:(c + 1) * 512], preferred_element_type=F32)

    def emit(val, c):
        qkv1_ref[:, c * ATTN_WIDTH:(c + 1) * ATTN_WIDTH] = val.astype(BF16)
        q4, q16 = tile // 4, tile // 16
        for s in range(SLABS):
            lanes = slice(s * LANES, (s + 1) * LANES)
            out_lanes = slice(c * ATTN_WIDTH + s * LANES, c * ATTN_WIDTH + (s + 1) * LANES)
            nat_scr[s] = val[:, lanes]
            for r1 in range(4):
                plane = nat_scr[s, pl.ds(r1, q4, stride=4), :]
                d4_scr[s, r1 * q4:(r1 + 1) * q4, :] = plane
                qkv4_ref[0, r1, :, out_lanes] = plane.astype(BF16)
            for r1 in range(4):
                for r2 in range(4):
                    plane = d4_scr[s, pl.ds(r1 * q4 + r2, q16, stride=4), :]
                    qkv16_ref[0, 4 * r2 + r1, :, out_lanes] = plane.astype(BF16)

    emit(proj(0) * (LOG2E / math.sqrt(HEAD_DIM)), 0)
    emit(proj(1), 1)
    emit(proj(2), 2)
    u = proj(3)
    gb_ref[...] = proj(4).astype(BF16)
    h_ref[...] = (proj(5) * u).astype(BF16)


def _inproj(x2d, w_in_bf16, batch, seq, tile):
    nt = x2d.shape[0]
    tps = seq // tile
    nat = jax.ShapeDtypeStruct((nt, 512), BF16)
    row_spec = pl.BlockSpec((tile, 512), lambda i: (i, 0))

    def plane_out(d):
        shape = jax.ShapeDtypeStruct((batch, d, seq // d, 3 * ATTN_WIDTH), BF16)
        spec = pl.BlockSpec((1, d, tile // d, 3 * ATTN_WIDTH), lambda i: (i // tps, 0, i % tps, 0))
        return shape, spec

    s4, b4 = plane_out(4)
    s16, b16 = plane_out(16)
    nat3 = jax.ShapeDtypeStruct((nt, 3 * ATTN_WIDTH), BF16)
    row3_spec = pl.BlockSpec((tile, 3 * ATTN_WIDTH), lambda i: (i, 0))
    return pl.pallas_call(
        functools.partial(_inproj_kernel, tile=tile),
        grid=(nt // tile,),
        in_specs=[pl.BlockSpec((tile, D_MODEL), lambda i: (i, 0)),
                  pl.BlockSpec((D_MODEL, 6 * 512), lambda i: (0, 0))],
        out_specs=[row3_spec, b4, b16] + [row_spec] * 2,
        out_shape=[nat3, s4, s16] + [nat] * 2,
        scratch_shapes=[pltpu.VMEM((SLABS, tile, LANES), F32)] * 2,
        compiler_params=pltpu.CompilerParams(dimension_semantics=("arbitrary",),
                                             vmem_limit_bytes=VMEM_LIMIT),
        name="inproj",
    )(x2d, w_in_bf16)


def _bias_tables(dilation):
    slopes = np.array([2.0 ** (-8.0 * (h + 1) / N_HEADS) for h in range(N_HEADS)], np.float64)
    i = np.arange(Q_BLOCK)[:, None]
    j = np.arange(K_BLOCK)[None, :]
    rel = np.abs(i + BAND_HALF - j)
    band = rel <= BAND_HALF
    dist = (rel * dilation).astype(np.float64)
    bias = -slopes[:, None, None] * dist[None] * LOG2E
    tabs = []
    for valid_cols in (j >= BAND_HALF, j >= 0, j < Q_BLOCK + BAND_HALF):
        tabs.append(np.where((band & valid_cols)[None], bias, NEG))
    return np.stack(tabs).astype(np.float32)


def _attn_kernel(q_ref, kp_ref, km_ref, kn_ref, vp_ref, vm_ref, vn_ref, bias_ref,
                 o_ref, lse_ref, *, tq, n_tiles):
    nb = tq // Q_BLOCK
    tile = pl.program_id(2)
    lane = lax.broadcasted_iota(jnp.int32, (1, LANES), 1)
    low = lane < HEAD_DIM

    def chunk(prev_ref, main_ref, next_ref, c):
        if c == 0:
            return jnp.concatenate([prev_ref[0, 0], main_ref[0, 0, 0:BAND_HALF, :]], axis=0)
        if c == nb:
            return jnp.concatenate([main_ref[0, 0, tq - BAND_HALF:tq, :], next_ref[0, 0]], axis=0)
        return main_ref[0, 0, c * Q_BLOCK - BAND_HALF:c * Q_BLOCK + BAND_HALF, :]

    kt, vc = [], []
    for c in range(nb + 1):
        kc = chunk(kp_ref, km_ref, kn_ref, c)
        kt.append([kc[:, g * LANES:(g + 1) * LANES].T for g in range(SLABS)])
        vc.append(chunk(vp_ref, vm_ref, vn_ref, c))
    ones = jnp.ones((K_BLOCK, LANES), BF16)

    for b in range(nb):
        rows = slice(b * Q_BLOCK, (b + 1) * Q_BLOCK)
        if b == 0:
            variant = jnp.where(tile == 0, 0, 1)
        elif b == nb - 1:
            variant = jnp.where(tile == n_tiles - 1, 2, 1)
        else:
            variant = 1
        m_tile = jnp.zeros((Q_BLOCK, LANES), F32)
        l_tile = jnp.ones((Q_BLOCK, LANES), F32)
        for g in range(SLABS):
            cols = slice(g * LANES, (g + 1) * LANES)
            qp = q_ref[0, 0, rows, cols]
            ktp = jnp.concatenate([kt[b][g], kt[b + 1][g]], axis=1)
            vpair = jnp.concatenate([vc[b][:, cols], vc[b + 1][:, cols]], axis=0)
            vaug = jnp.concatenate([vpair, ones], axis=1)
            outs = []
            for e in range(2):
                h = 2 * g + e
                qe = jnp.where(low if e == 0 else jnp.logical_not(low), qp, jnp.zeros_like(qp))
                s = jnp.dot(qe, ktp, preferred_element_type=F32) + bias_ref[variant, h]
                m = jnp.max(s, axis=-1, keepdims=True)
                p = jnp.exp2(s - m)
                pv = jnp.dot(p.astype(BF16), vaug, preferred_element_type=F32)
                l = pv[:, LANES:]
                outs.append(pv[:, :LANES] / l)
                m_tile = jnp.where(lane == h, m, m_tile)
                l_tile = jnp.where(lane == h, l, l_tile)
            o_ref[0, 0, rows, cols] = jnp.where(low, outs[0], outs[1]).astype(BF16)
        lse_ref[0, 0, rows, :] = (m_tile + jnp.log2(l_tile)) * LN2


def _whole_plane_bias_tables(dilation):
    slopes = np.array([2.0 ** (-8.0 * (h + 1) / N_HEADS) for h in range(N_HEADS)], np.float64)
    i = np.arange(Q_BLOCK)[:, None]
    j = np.arange(2 * Q_BLOCK)[None, :]
    tabs = []
    for b in range(2):
        rel = np.abs(b * Q_BLOCK + i - j)
        bias = -slopes[:, None, None] * (rel * dilation).astype(np.float64)[None] * LOG2E
        tabs.append(np.where((rel <= BAND_HALF)[None], bias, NEG))
    return np.stack(tabs).astype(np.float32)


def _whole_plane_attn_kernel(q_ref, k_ref, v_ref, bias_ref, o_ref, lse_ref, *, planes):
    lane = lax.broadcasted_iota(jnp.int32, (1, LANES), 1)
    low = lane < HEAD_DIM
    ones = jnp.ones((2 * Q_BLOCK, LANES), BF16)
    for r in range(planes):
        m_tiles = [jnp.zeros((Q_BLOCK, LANES), F32)] * 2
        l_tiles = [jnp.ones((Q_BLOCK, LANES), F32)] * 2
        for g in range(SLABS):
            cols = slice(g * LANES, (g + 1) * LANES)
            ktp = jnp.concatenate([k_ref[0, r, 0:Q_BLOCK, cols].T, k_ref[0, r, Q_BLOCK:2 * Q_BLOCK, cols].T],
                                  axis=1)
            vaug = jnp.concatenate([v_ref[0, r, :, cols], ones], axis=1)
            for b in range(2):
                rows = slice(b * Q_BLOCK, (b + 1) * Q_BLOCK)
                qp = q_ref[0, r, rows, cols]
                outs = []
                for e in range(2):
                    h = 2 * g + e
                    qe = jnp.where(low if e == 0 else jnp.logical_not(low), qp, jnp.zeros_like(qp))
                    s = jnp.dot(qe, ktp, preferred_element_type=F32) + bias_ref[b, h]
                    m = jnp.max(s, axis=-1, keepdims=True)
                    p = jnp.exp2(s - m)
                    pv = jnp.dot(p.astype(BF16), vaug, preferred_element_type=F32)
                    l = pv[:, LANES:]
                    outs.append(pv[:, :LANES] / l)
                    m_tiles[b] = jnp.where(lane == h, m, m_tiles[b])
                    l_tiles[b] = jnp.where(lane == h, l, l_tiles[b])
                o_ref[0, r, rows, cols] = jnp.where(low, outs[0], outs[1]).astype(BF16)
        for b in range(2):
            lse_ref[0, r, b * Q_BLOCK:(b + 1) * Q_BLOCK, :] = (m_tiles[b] + jnp.log2(l_tiles[b])) * LN2


def _whole_plane_attention(qkv, planes):
    batch, d, n, _ = qkv.shape
    blk = lambda width, c=0: pl.BlockSpec((1, planes, n, width), lambda b, r: (b, r, 0, c))
    return pl.pallas_call(
        functools.partial(_whole_plane_attn_kernel, planes=planes),
        grid=(batch, d // planes),
        in_specs=[blk(ATTN_WIDTH, c) for c in range(3)]
                 + [pl.BlockSpec((2, N_HEADS, Q_BLOCK, 2 * Q_BLOCK), lambda b, r: (0, 0, 0, 0))],
        out_specs=[blk(ATTN_WIDTH), blk(LANES)],
        out_shape=[jax.ShapeDtypeStruct((batch, d, n, ATTN_WIDTH), BF16),
                   jax.ShapeDtypeStruct((batch, d, n, LANES), F32)],
        compiler_params=pltpu.CompilerParams(dimension_semantics=("arbitrary",) * 2,
                                             vmem_limit_bytes=VMEM_LIMIT),
        name=f"band_attn_whole_d{d}",
    )(qkv, qkv, qkv, jnp.asarray(_whole_plane_bias_tables(d)))


def _band_attention(qkv, dilation):
    batch, d, n, _ = qkv.shape
    assert d == dilation and n % Q_BLOCK == 0 and n >= 2 * Q_BLOCK
    if n == 2 * Q_BLOCK:
        return _whole_plane_attention(qkv, planes=math.gcd(d, 4))
    tq = min(n, MAX_ATTN_TILE)
    halo_per_tile = tq // BAND_HALF
    n_halo_blocks = n // BAND_HALF
    bias = jnp.asarray(_bias_tables(d))

    main = lambda c: pl.BlockSpec((1, 1, tq, ATTN_WIDTH), lambda b, r, i: (b, r, i, c))
    prev = lambda c: pl.BlockSpec((1, 1, BAND_HALF, ATTN_WIDTH),
                                  lambda b, r, i: (b, r, jnp.maximum(i * halo_per_tile - 1, 0), c))
    nxt = lambda c: pl.BlockSpec(
        (1, 1, BAND_HALF, ATTN_WIDTH),
        lambda b, r, i: (b, r, jnp.minimum((i + 1) * halo_per_tile, n_halo_blocks - 1), c))
    return pl.pallas_call(
        functools.partial(_attn_kernel, tq=tq, n_tiles=n // tq),
        grid=(batch, d, n // tq),
        in_specs=[main(0), prev(1), main(1), nxt(1), prev(2), main(2), nxt(2),
                  pl.BlockSpec((3, N_HEADS, Q_BLOCK, K_BLOCK), lambda b, r, i: (0, 0, 0, 0))],
        out_specs=[main(0), pl.BlockSpec((1, 1, tq, LANES), lambda b, r, i: (b, r, i, 0))],
        out_shape=[jax.ShapeDtypeStruct((batch, d, n, ATTN_WIDTH), BF16),
                   jax.ShapeDtypeStruct((batch, d, n, LANES), F32)],
        compiler_params=pltpu.CompilerParams(dimension_semantics=("arbitrary",) * 3,
                                             vmem_limit_bytes=VMEM_LIMIT),
        name=f"band_attn_d{d}",
    )(*([qkv] * 7), bias)


def _layernorm(x, g, b):
    mu = jnp.mean(x, axis=-1, keepdims=True)
    xc = x - mu
    var = jnp.mean(xc * xc, axis=-1, keepdims=True)
    return xc * lax.rsqrt(var + LN_EPS) * g + b


def _rmsnorm(x, g):
    return x * lax.rsqrt(jnp.mean(x * x, axis=-1, keepdims=True) + LN_EPS) * g


def _post_kernel(x_ref, o1_ref, o4_ref, o16_ref, l1_ref, l4_ref, l16_ref,
                 h_ref, hprev_ref, hnext_ref, gb_ref, convw_ref, gattn_ref, gconv_ref, wo_ref,
                 ln1g_ref, ln1b_ref, wg_ref, wu_ref, wd_ref, ln2g_ref, ln2b_ref, out_ref,
                 mix_scr, o4_scr, o16_scr, l4_scr, l16_scr, *, tile, tiles_per_seq, n_tiles):
    j = pl.program_id(0)

    @pl.when(j == 0)
    def _():
        mix_scr[...] = jnp.zeros_like(mix_scr)

    mix = jnp.dot(mix_scr[...], wo_ref[...], preferred_element_type=F32)
    x1 = _layernorm(ALPHA * x_ref[...] + mix, ln1g_ref[...], ln1b_ref[...])
    x1b = x1.astype(BF16)
    ffn = None
    for c0, cw_ in FFN_CHUNKS:
        gate = jnp.dot(x1b, wg_ref[:, c0:c0 + cw_], preferred_element_type=F32)
        up = jnp.dot(x1b, wu_ref[:, c0:c0 + cw_], preferred_element_type=F32)
        act = (gate * (1.0 / (1.0 + jnp.exp(-gate))) * up).astype(BF16)
        part = jnp.dot(act, wd_ref[c0:c0 + cw_, :], preferred_element_type=F32)
        ffn = part if ffn is None else ffn + part
    out_ref[...] = _layernorm(ALPHA * x1 + ffn, ln2g_ref[...], ln2b_ref[...])

    lane = lax.broadcasted_iota(jnp.int32, (1, LANES), 1)

    def interleave(src_ref, dst_scr, d, slabs):
        for r in range(d):
            for s in range(slabs):
                dst_scr[s, pl.ds(r, tile // d, stride=d), :] = \
                    src_ref[0, r, :, s * LANES:(s + 1) * LANES].astype(F32)

    interleave(o4_ref, o4_scr, 4, SLABS)
    interleave(o16_ref, o16_scr, 16, SLABS)
    interleave(l4_ref, l4_scr, 4, 1)
    interleave(l16_ref, l16_scr, 16, 1)

    lses = (l1_ref[...], l4_scr[0], l16_scr[0])
    top = jnp.maximum(jnp.maximum(lses[0], lses[1]), lses[2])
    es = [jnp.exp(l - top) for l in lses]
    inv = 1.0 / (es[0] + es[1] + es[2])
    ws = [e * inv for e in es]
    pairs = []
    for g in range(SLABS):
        cols = slice(g * LANES, (g + 1) * LANES)
        os_ = (o1_ref[:, cols].astype(F32), o4_scr[g], o16_scr[g])
        acc = None
        for w, o in zip(ws, os_):
            wpair = jnp.where(lane < HEAD_DIM, w[:, 2 * g:2 * g + 1], w[:, 2 * g + 1:2 * g + 2])
            acc = wpair * o if acc is None else acc + wpair * o
        pairs.append(acc)
    attn = jnp.concatenate(pairs, axis=-1)

    hf = h_ref[...].astype(F32)
    pos = jnp.minimum(j, n_tiles - 1) % tiles_per_seq
    prev_row = jnp.where(pos == 0, 0.0, hprev_ref[HALO_ROWS_BF16 - 1:HALO_ROWS_BF16, :].astype(F32))
    next_row = jnp.where(pos == tiles_per_seq - 1, 0.0, hnext_ref[0:1, :].astype(F32))
    row = lax.broadcasted_iota(jnp.int32, (tile, 1), 0)
    h_m1 = jnp.where(row == 0, prev_row, pltpu.roll(hf, 1, 0))
    h_p1 = jnp.where(row == tile - 1, next_row, pltpu.roll(hf, tile - 1, 0))
    cw = convw_ref[...]
    conv = gb_ref[...].astype(F32) * (cw[0:1, :] * h_m1 + cw[1:2, :] * hf + cw[2:3, :] * h_p1)

    mix_scr[:, 0:ATTN_WIDTH] = _rmsnorm(attn, gattn_ref[...]).astype(BF16)
    mix_scr[:, ATTN_WIDTH:D_MODEL] = _rmsnorm(conv, gconv_ref[...]).astype(BF16)


def _post(x2d, os_, lses, h, gb, params, seq, tile):
    nt = x2d.shape[0]
    n_tiles = nt // tile
    tps = seq // tile
    halo_per_tile = tile // HALO_ROWS_BF16
    n_halo = nt // HALO_ROWS_BF16
    cur = lambda j: jnp.minimum(j, n_tiles - 1)
    old = lambda j: jnp.maximum(j - 1, 0)
    row = lambda width: pl.BlockSpec((tile, width), lambda j: (cur(j), 0))
    planes = lambda d, width: pl.BlockSpec((1, d, tile // d, width),
                                           lambda j: (cur(j) // tps, 0, cur(j) % tps, 0))
    const = lambda shape: pl.BlockSpec(shape, lambda j: (0,) * len(shape), pipeline_mode=pl.Buffered(1))
    hprev = pl.BlockSpec((HALO_ROWS_BF16, CONV_WIDTH),
                         lambda j: (jnp.maximum(cur(j) * halo_per_tile - 1, 0), 0))
    hnext = pl.BlockSpec((HALO_ROWS_BF16, CONV_WIDTH),
                         lambda j: (jnp.minimum((cur(j) + 1) * halo_per_tile, n_halo - 1), 0))
    (conv_w, g_attn, g_conv, w_o, ln1_g, ln1_b, w_gate, w_up, w_down, ln2_g, ln2_b) = params
    vec = lambda a: a.reshape(1, -1).astype(F32)
    return pl.pallas_call(
        functools.partial(_post_kernel, tile=tile, tiles_per_seq=tps, n_tiles=n_tiles),
        grid=(n_tiles + 1,),
        in_specs=[pl.BlockSpec((tile, D_MODEL), lambda j: (old(j), 0)),
                  row(ATTN_WIDTH), planes(4, ATTN_WIDTH), planes(16, ATTN_WIDTH),
                  row(LANES), planes(4, LANES), planes(16, LANES),
                  row(CONV_WIDTH), hprev, hnext, row(CONV_WIDTH),
                  const((3, CONV_WIDTH)), const((1, ATTN_WIDTH)), const((1, CONV_WIDTH)),
                  const((D_MODEL, D_MODEL)), const((1, D_MODEL)), const((1, D_MODEL)),
                  const((D_MODEL, D_FF)), const((D_MODEL, D_FF)), const((D_FF, D_MODEL)),
                  const((1, D_MODEL)), const((1, D_MODEL))],
        out_specs=pl.BlockSpec((tile, D_MODEL), lambda j: (old(j), 0)),
        out_shape=jax.ShapeDtypeStruct((nt, D_MODEL), F32),
        scratch_shapes=[pltpu.VMEM((tile, D_MODEL), BF16),
                        pltpu.VMEM((SLABS, tile, LANES), F32), pltpu.VMEM((SLABS, tile, LANES), F32),
                        pltpu.VMEM((1, tile, LANES), F32), pltpu.VMEM((1, tile, LANES), F32)],
        compiler_params=pltpu.CompilerParams(dimension_semantics=("arbitrary",),
                                             vmem_limit_bytes=VMEM_LIMIT),
        name="post",
    )(x2d, *os_, *lses, h, h, h, gb, conv_w.astype(F32), vec(g_attn), vec(g_conv), w_o.astype(BF16),
      vec(ln1_g), vec(ln1_b), w_gate.astype(BF16), w_up.astype(BF16), w_down.astype(BF16),
      vec(ln2_g), vec(ln2_b))


def _layer(x, w_in_bf16, params):
    batch, seq, _ = x.shape
    nt = batch * seq
    x2d = x.reshape(nt, D_MODEL)
    qkv1, qkv4, qkv16, h, gb = _inproj(x2d, w_in_bf16, batch, seq, tile=1024)
    o1, l1 = _band_attention(qkv1.reshape(batch, 1, seq, 3 * ATTN_WIDTH), 1)
    o4, l4 = _band_attention(qkv4, 4)
    o16, l16 = _band_attention(qkv16, 16)
    y = _post(x2d, (o1.reshape(nt, ATTN_WIDTH), o4, o16), (l1.reshape(nt, LANES), l4, l16),
              h, gb, params, seq, tile=512)
    return y.reshape(batch, seq, D_MODEL)


def kernel(x_prompt, x_sample, w_in, conv_w, g_attn, g_conv, w_o, ln1_g, ln1_b, w_gate, w_up, w_down, ln2_g, ln2_b):
    w_in_bf16 = w_in.astype(BF16)
    params = (conv_w, g_attn, g_conv, w_o, ln1_g, ln1_b, w_gate, w_up, w_down, ln2_g, ln2_b)
    y_prompt = x_prompt
    y_sample = x_sample
    for _ in range(DEPTH):
        y_prompt = _layer(y_prompt, w_in_bf16, params)
        y_sample = _layer(y_sample, w_in_bf16, params)
    return (y_prompt, y_sample)
```

```python
import functools
import math

import numpy as np
import jax
import jax.numpy as jnp
from jax import lax
from jax.experimental import pallas as pl
from jax.experimental.pallas import tpu as pltpu

D_MODEL = 1024
ATTN_WIDTH = 512
CONV_WIDTH = 512
HEAD_DIM = 64
N_HEADS = 8
ATTN_CONFIGS = ((128, 1), (512, 4), (2048, 16))
BAND_HALF = 64
D_FF = 2816
DEPTH = 1
ALPHA = (2.0 * DEPTH) ** 0.25
LN_EPS = 1e-5
NEG = -1e30
LOG2E = math.log2(math.e)
LN2 = math.log(2.0)

LANES = 128
SLABS = ATTN_WIDTH // LANES
Q_BLOCK = 128
K_BLOCK = Q_BLOCK + 2 * BAND_HALF
MAX_ATTN_TILE = 2048
HALO_ROWS_BF16 = 16
FFN_CHUNKS = ((0, 1024), (1024, 1024), (2048, 768))
VMEM_LIMIT = 56 * 1024 * 1024

F32 = jnp.float32
BF16 = jnp.bfloat16


def _inproj_kernel(x_ref, w_ref, qkv1_ref, qkv4_ref, qkv16_ref, h_ref, gb_ref, nat_scr, d4_scr, *, tile):
    xb = x_ref[...].astype(BF16)

    def proj(c):
        return jnp.dot(xb, w_ref[:, c * 512:(c + 1) * 512], preferred_element_type=F32)

    def emit(val, c):
        qkv1_ref[:, c * ATTN_WIDTH:(c + 1) * ATTN_WIDTH] = val.astype(BF16)
        q4, q16 = tile // 4, tile // 16
        for s in range(SLABS):
            lanes = slice(s * LANES, (s + 1) * LANES)
            out_lanes = slice(c * ATTN_WIDTH + s * LANES, c * ATTN_WIDTH + (s + 1) * LANES)
            nat_scr[s] = val[:, lanes]
            for r1 in range(4):
                plane = nat_scr[s, pl.ds(r1, q4, stride=4), :]
                d4_scr[s, r1 * q4:(r1 + 1) * q4, :] = plane
                qkv4_ref[0, r1, :, out_lanes] = plane.astype(BF16)
            for r1 in range(4):
                for r2 in range(4):
                    plane = d4_scr[s, pl.ds(r1 * q4 + r2, q16, stride=4), :]
                    qkv16_ref[0, 4 * r2 + r1, :, out_lanes] = plane.astype(BF16)

    emit(proj(0) * (LOG2E / math.sqrt(HEAD_DIM)), 0)
    emit(proj(1), 1)
    emit(proj(2), 2)
    u = proj(3)
    gb_ref[...] = proj(4).astype(BF16)
    h_ref[...] = (proj(5) * u).astype(BF16)


def _inproj(x2d, w_in_bf16, batch, seq, tile):
    nt = x2d.shape[0]
    tps = seq // tile
    nat = jax.ShapeDtypeStruct((nt, 512), BF16)
    row_spec = pl.BlockSpec((tile, 512), lambda i: (i, 0))

    def plane_out(d):
        shape = jax.ShapeDtypeStruct((batch, d, seq // d, 3 * ATTN_WIDTH), BF16)
        spec = pl.BlockSpec((1, d, tile // d, 3 * ATTN_WIDTH), lambda i: (i // tps, 0, i % tps, 0))
        return shape, spec

    s4, b4 = plane_out(4)
    s16, b16 = plane_out(16)
    nat3 = jax.ShapeDtypeStruct((nt, 3 * ATTN_WIDTH), BF16)
    row3_spec = pl.BlockSpec((tile, 3 * ATTN_WIDTH), lambda i: (i, 0))
    return pl.pallas_call(
        functools.partial(_inproj_kernel, tile=tile),
        grid=(nt // tile,),
        in_specs=[pl.BlockSpec((tile, D_MODEL), lambda i: (i, 0)),
                  pl.BlockSpec((D_MODEL, 6 * 512), lambda i: (0, 0))],
        out_specs=[row3_spec, b4, b16] + [row_spec] * 2,
        out_shape=[nat3, s4, s16] + [nat] * 2,
        scratch_shapes=[pltpu.VMEM((SLABS, tile, LANES), F32)] * 2,
        compiler_params=pltpu.CompilerParams(dimension_semantics=("arbitrary",),
                                             vmem_limit_bytes=VMEM_LIMIT),
        name="inproj",
    )(x2d, w_in_bf16)


def _bias_tables(dilation):
    slopes = np.array([2.0 ** (-8.0 * (h + 1) / N_HEADS) for h in range(N_HEADS)], np.float64)
    i = np.arange(Q_BLOCK)[:, None]
    j = np.arange(K_BLOCK)[None, :]
    rel = np.abs(i + BAND_HALF - j)
    band = rel <= BAND_HALF
    dist = (rel * dilation).astype(np.float64)
    bias = -slopes[:, None, None] * dist[None] * LOG2E
    tabs = []
    for valid_cols in (j >= BAND_HALF, j >= 0, j < Q_BLOCK + BAND_HALF):
        tabs.append(np.where((band & valid_cols)[None], bias, NEG))
    return np.stack(tabs).astype(np.float32)


def _attn_kernel(q_ref, kp_ref, km_ref, kn_ref, vp_ref, vm_ref, vn_ref, bias_ref,
                 o_ref, lse_ref, *, tq, n_tiles):
    nb = tq // Q_BLOCK
    tile = pl.program_id(2)
    lane = lax.broadcasted_iota(jnp.int32, (1, LANES), 1)
    low = lane < HEAD_DIM

    def chunk(prev_ref, main_ref, next_ref, c):
        if c == 0:
            return jnp.concatenate([prev_ref[0, 0], main_ref[0, 0, 0:BAND_HALF, :]], axis=0)
        if c == nb:
            return jnp.concatenate([main_ref[0, 0, tq - BAND_HALF:tq, :], next_ref[0, 0]], axis=0)
        return main_ref[0, 0, c * Q_BLOCK - BAND_HALF:c * Q_BLOCK + BAND_HALF, :]

    kt, vc = [], []
    for c in range(nb + 1):
        kc = chunk(kp_ref, km_ref, kn_ref, c)
        kt.append([kc[:, g * LANES:(g + 1) * LANES].T for g in range(SLABS)])
        vc.append(chunk(vp_ref, vm_ref, vn_ref, c))
    ones = jnp.ones((K_BLOCK, LANES), BF16)

    for b in range(nb):
        rows = slice(b * Q_BLOCK, (b + 1) * Q_BLOCK)
        if b == 0:
            variant = jnp.where(tile == 0, 0, 1)
        elif b == nb - 1:
            variant = jnp.where(tile == n_tiles - 1, 2, 1)
        else:
            variant = 1
        m_tile = jnp.zeros((Q_BLOCK, LANES), F32)
        l_tile = jnp.ones((Q_BLOCK, LANES), F32)
        for g in range(SLABS):
            cols = slice(g * LANES, (g + 1) * LANES)
            qp = q_ref[0, 0, rows, cols]
            ktp = jnp.concatenate([kt[b][g], kt[b + 1][g]], axis=1)
            vpair = jnp.concatenate([vc[b][:, cols], vc[b + 1][:, cols]], axis=0)
            vaug = jnp.concatenate([vpair, ones], axis=1)
            outs = []
            for e in range(2):
                h = 2 * g + e
                qe = jnp.where(low if e == 0 else jnp.logical_not(low), qp, jnp.zeros_like(qp))
                s = jnp.dot(qe, ktp, preferred_element_type=F32) + bias_ref[variant, h]
                m = jnp.max(s, axis=-1, keepdims=True)
                p = jnp.exp2(s - m)
                pv = jnp.dot(p.astype(BF16), vaug, preferred_element_type=F32)
                l = pv[:, LANES:]
                outs.append(pv[:, :LANES] / l)
                m_tile = jnp.where(lane == h, m, m_tile)
                l_tile = jnp.where(lane == h, l, l_tile)
            o_ref[0, 0, rows, cols] = jnp.where(low, outs[0], outs[1]).astype(BF16)
        lse_ref[0, 0, rows, :] = (m_tile + jnp.log2(l_tile)) * LN2


def _whole_plane_bias_tables(dilation):
    slopes = np.array([2.0 ** (-8.0 * (h + 1) / N_HEADS) for h in range(N_HEADS)], np.float64)
    i = np.arange(Q_BLOCK)[:, None]
    j = np.arange(2 * Q_BLOCK)[None, :]
    tabs = []
    for b in range(2):
        rel = np.abs(b * Q_BLOCK + i - j)
        bias = -slopes[:, None, None] * (rel * dilation).astype(np.float64)[None] * LOG2E
        tabs.append(np.where((rel <= BAND_HALF)[None], bias, NEG))
    return np.stack(tabs).astype(np.float32)


def _whole_plane_attn_kernel(q_ref, k_ref, v_ref, bias_ref, o_ref, lse_ref, *, planes):
    lane = lax.broadcasted_iota(jnp.int32, (1, LANES), 1)
    low = lane < HEAD_DIM
    ones = jnp.ones((2 * Q_BLOCK, LANES), BF16)
    for r in range(planes):
        m_tiles = [jnp.zeros((Q_BLOCK, LANES), F32)] * 2
        l_tiles = [jnp.ones((Q_BLOCK, LANES), F32)] * 2
        for g in range(SLABS):
            cols = slice(g * LANES, (g + 1) * LANES)
            ktp = jnp.concatenate([k_ref[0, r, 0:Q_BLOCK, cols].T, k_ref[0, r, Q_BLOCK:2 * Q_BLOCK, cols].T],
                                  axis=1)
            vaug = jnp.concatenate([v_ref[0, r, :, cols], ones], axis=1)
            for b in range(2):
                rows = slice(b * Q_BLOCK, (b + 1) * Q_BLOCK)
                qp = q_ref[0, r, rows, cols]
                outs = []
                for e in range(2):
                    h = 2 * g + e
                    qe = jnp.where(low if e == 0 else jnp.logical_not(low), qp, jnp.zeros_like(qp))
                    s = jnp.dot(qe, ktp, preferred_element_type=F32) + bias_ref[b, h]
                    m = jnp.max(s, axis=-1, keepdims=True)
                    p = jnp.exp2(s - m)
                    pv = jnp.dot(p.astype(BF16), vaug, preferred_element_type=F32)
                    l = pv[:, LANES:]
                    outs.append(pv[:, :LANES] / l)
                    m_tiles[b] = jnp.where(lane == h, m, m_tiles[b])
                    l_tiles[b] = jnp.where(lane == h, l, l_tiles[b])
                o_ref[0, r, rows, cols] = jnp.where(low, outs[0], outs[1]).astype(BF16)
        for b in range(2):
            lse_ref[0, r, b * Q_BLOCK:(b + 1) * Q_BLOCK, :] = (m_tiles[b] + jnp.log2(l_tiles[b])) * LN2


def _whole_plane_attention(qkv, planes):
    batch, d, n, _ = qkv.shape
    blk = lambda width, c=0: pl.BlockSpec((1, planes, n, width), lambda b, r: (b, r, 0, c))
    return pl.pallas_call(
        functools.partial(_whole_plane_attn_kernel, planes=planes),
        grid=(batch, d // planes),
        in_specs=[blk(ATTN_WIDTH, c) for c in range(3)]
                 + [pl.BlockSpec((2, N_HEADS, Q_BLOCK, 2 * Q_BLOCK), lambda b, r: (0, 0, 0, 0))],
        out_specs=[blk(ATTN_WIDTH), blk(LANES)],
        out_shape=[jax.ShapeDtypeStruct((batch, d, n, ATTN_WIDTH), BF16),
                   jax.ShapeDtypeStruct((batch, d, n, LANES), F32)],
        compiler_params=pltpu.CompilerParams(dimension_semantics=("arbitrary",) * 2,
                                             vmem_limit_bytes=VMEM_LIMIT),
        name=f"band_attn_whole_d{d}",
    )(qkv, qkv, qkv, jnp.asarray(_whole_plane_bias_tables(d)))


def _band_attention(qkv, dilation):
    batch, d, n, _ = qkv.shape
    assert d == dilation and n % Q_BLOCK == 0 and n >= 2 * Q_BLOCK
    if n == 2 * Q_BLOCK:
        return _whole_plane_attention(qkv, planes=math.gcd(d, 4))
    tq = min(n, MAX_ATTN_TILE)
    halo_per_tile = tq // BAND_HALF
    n_halo_blocks = n // BAND_HALF
    bias = jnp.asarray(_bias_tables(d))

    main = lambda c: pl.BlockSpec((1, 1, tq, ATTN_WIDTH), lambda b, r, i: (b, r, i, c))
    prev = lambda c: pl.BlockSpec((1, 1, BAND_HALF, ATTN_WIDTH),
                                  lambda b, r, i: (b, r, jnp.maximum(i * halo_per_tile - 1, 0), c))
    nxt = lambda c: pl.BlockSpec(
        (1, 1, BAND_HALF, ATTN_WIDTH),
        lambda b, r, i: (b, r, jnp.minimum((i + 1) * halo_per_tile, n_halo_blocks - 1), c))
    return pl.pallas_call(
        functools.partial(_attn_kernel, tq=tq, n_tiles=n // tq),
        grid=(batch, d, n // tq),
        in_specs=[main(0), prev(1), main(1), nxt(1), prev(2), main(2), nxt(2),
                  pl.BlockSpec((3, N_HEADS, Q_BLOCK, K_BLOCK), lambda b, r, i: (0, 0, 0, 0))],
        out_specs=[main(0), pl.BlockSpec((1, 1, tq, LANES), lambda b, r, i: (b, r, i, 0))],
        out_shape=[jax.ShapeDtypeStruct((batch, d, n, ATTN_WIDTH), BF16),
                   jax.ShapeDtypeStruct((batch, d, n, LANES), F32)],
        compiler_params=pltpu.CompilerParams(dimension_semantics=("arbitrary",) * 3,
                                             vmem_limit_bytes=VMEM_LIMIT),
        name=f"band_attn_d{d}",
    )(*([qkv] * 7), bias)


def _layernorm(x, g, b):
    mu = jnp.mean(x, axis=-1, keepdims=True)
    xc = x - mu
    var = jnp.mean(xc * xc, axis=-1, keepdims=True)
    return xc * lax.rsqrt(var + LN_EPS) * g + b


def _rmsnorm(x, g):
    return x * lax.rsqrt(jnp.mean(x * x, axis=-1, keepdims=True) + LN_EPS) * g


def _post_kernel(x_ref, o1_ref, o4_ref, o16_ref, l1_ref, l4_ref, l16_ref,
                 h_ref, hprev_ref, hnext_ref, gb_ref, convw_ref, gattn_ref, gconv_ref, wo_ref,
                 ln1g_ref, ln1b_ref, wg_ref, wu_ref, wd_ref, ln2g_ref, ln2b_ref, expand_ref, out_ref,
                 mix_scr, o4_scr, o16_scr, l4_scr, l16_scr, *, tile, tiles_per_seq, n_tiles):
    j = pl.program_id(0)

    @pl.when(j == 0)
    def _():
        mix_scr[...] = jnp.zeros_like(mix_scr)

    mix = jnp.dot(mix_scr[...], wo_ref[...], preferred_element_type=F32)
    x1 = _layernorm(ALPHA * x_ref[...] + mix, ln1g_ref[...], ln1b_ref[...])
    x1b = x1.astype(BF16)
    ffn = None
    for c0, cw_ in FFN_CHUNKS:
        gate = jnp.dot(x1b, wg_ref[:, c0:c0 + cw_], preferred_element_type=F32)
        up = jnp.dot(x1b, wu_ref[:, c0:c0 + cw_], preferred_element_type=F32)
        act = (gate * (1.0 / (1.0 + jnp.exp(-gate))) * up).astype(BF16)
        part = jnp.dot(act, wd_ref[c0:c0 + cw_, :], preferred_element_type=F32)
        ffn = part if ffn is None else ffn + part
    out_ref[...] = _layernorm(ALPHA * x1 + ffn, ln2g_ref[...], ln2b_ref[...])

    lane = lax.broadcasted_iota(jnp.int32, (1, LANES), 1)

    def interleave(src_ref, dst_scr, d, slabs):
        for r in range(d):
            for s in range(slabs):
                dst_scr[s, pl.ds(r, tile // d, stride=d), :] = \
                    src_ref[0, r, :, s * LANES:(s + 1) * LANES].astype(F32)

    interleave(o4_ref, o4_scr, 4, SLABS)
    interleave(o16_ref, o16_scr, 16, SLABS)
    interleave(l4_ref, l4_scr, 4, 1)
    interleave(l16_ref, l16_scr, 16, 1)

    lses = (l1_ref[...], l4_scr[0], l16_scr[0])
    top = jnp.maximum(jnp.maximum(lses[0], lses[1]), lses[2])
    es = [jnp.exp(l - top) for l in lses]
    inv = 1.0 / (es[0] + es[1] + es[2])
    wexp = []
    for e in es:
        w = e * inv
        hi = w.astype(BF16)
        lo = (w - hi.astype(F32)).astype(BF16)
        wexp.append(jnp.dot(jnp.concatenate([hi, lo], axis=1), expand_ref[...], preferred_element_type=F32))
    pairs = []
    for g in range(SLABS):
        cols = slice(g * LANES, (g + 1) * LANES)
        os_ = (o1_ref[:, cols].astype(F32), o4_scr[g], o16_scr[g])
        acc = None
        for w, o in zip(wexp, os_):
            acc = w[:, cols] * o if acc is None else acc + w[:, cols] * o
        pairs.append(acc)
    attn = jnp.concatenate(pairs, axis=-1)

    hf = h_ref[...].astype(F32)
    pos = jnp.minimum(j, n_tiles - 1) % tiles_per_seq
    prev_row = jnp.where(pos == 0, 0.0, hprev_ref[HALO_ROWS_BF16 - 1:HALO_ROWS_BF16, :].astype(F32))
    next_row = jnp.where(pos == tiles_per_seq - 1, 0.0, hnext_ref[0:1, :].astype(F32))
    row = lax.broadcasted_iota(jnp.int32, (tile, 1), 0)
    h_m1 = jnp.where(row == 0, prev_row, pltpu.roll(hf, 1, 0))
    h_p1 = jnp.where(row == tile - 1, next_row, pltpu.roll(hf, tile - 1, 0))
    cw = convw_ref[...]
    conv = gb_ref[...].astype(F32) * (cw[0:1, :] * h_m1 + cw[1:2, :] * hf + cw[2:3, :] * h_p1)

    mix_scr[:, 0:ATTN_WIDTH] = _rmsnorm(attn, gattn_ref[...]).astype(BF16)
    mix_scr[:, ATTN_WIDTH:D_MODEL] = _rmsnorm(conv, gconv_ref[...]).astype(BF16)


def _head_expansion():
    e = np.zeros((2 * LANES, ATTN_WIDTH), np.float32)
    for h in range(N_HEADS):
        e[h, h * HEAD_DIM:(h + 1) * HEAD_DIM] = 1.0
        e[LANES + h, h * HEAD_DIM:(h + 1) * HEAD_DIM] = 1.0
    return e


def _post(x2d, os_, lses, h, gb, params, seq, tile):
    nt = x2d.shape[0]
    n_tiles = nt // tile
    tps = seq // tile
    halo_per_tile = tile // HALO_ROWS_BF16
    n_halo = nt // HALO_ROWS_BF16
    cur = lambda j: jnp.minimum(j, n_tiles - 1)
    old = lambda j: jnp.maximum(j - 1, 0)
    row = lambda width: pl.BlockSpec((tile, width), lambda j: (cur(j), 0))
    planes = lambda d, width: pl.BlockSpec((1, d, tile // d, width),
                                           lambda j: (cur(j) // tps, 0, cur(j) % tps, 0))
    const = lambda shape: pl.BlockSpec(shape, lambda j: (0,) * len(shape), pipeline_mode=pl.Buffered(1))
    hprev = pl.BlockSpec((HALO_ROWS_BF16, CONV_WIDTH),
                         lambda j: (jnp.maximum(cur(j) * halo_per_tile - 1, 0), 0))
    hnext = pl.BlockSpec((HALO_ROWS_BF16, CONV_WIDTH),
                         lambda j: (jnp.minimum((cur(j) + 1) * halo_per_tile, n_halo - 1), 0))
    (conv_w, g_attn, g_conv, w_o, ln1_g, ln1_b, w_gate, w_up, w_down, ln2_g, ln2_b) = params
    vec = lambda a: a.reshape(1, -1).astype(F32)
    return pl.pallas_call(
        functools.partial(_post_kernel, tile=tile, tiles_per_seq=tps, n_tiles=n_tiles),
        grid=(n_tiles + 1,),
        in_specs=[pl.BlockSpec((tile, D_MODEL), lambda j: (old(j), 0)),
                  row(ATTN_WIDTH), planes(4, ATTN_WIDTH), planes(16, ATTN_WIDTH),
                  row(LANES), planes(4, LANES), planes(16, LANES),
                  row(CONV_WIDTH), hprev, hnext, row(CONV_WIDTH),
                  const((3, CONV_WIDTH)), const((1, ATTN_WIDTH)), const((1, CONV_WIDTH)),
                  const((D_MODEL, D_MODEL)), const((1, D_MODEL)), const((1, D_MODEL)),
                  const((D_MODEL, D_FF)), const((D_MODEL, D_FF)), const((D_FF, D_MODEL)),
                  const((1, D_MODEL)), const((1, D_MODEL)), const((2 * LANES, ATTN_WIDTH))],
        out_specs=pl.BlockSpec((tile, D_MODEL), lambda j: (old(j), 0)),
        out_shape=jax.ShapeDtypeStruct((nt, D_MODEL), F32),
        scratch_shapes=[pltpu.VMEM((tile, D_MODEL), BF16),
                        pltpu.VMEM((SLABS, tile, LANES), F32), pltpu.VMEM((SLABS, tile, LANES), F32),
                        pltpu.VMEM((1, tile, LANES), F32), pltpu.VMEM((1, tile, LANES), F32)],
        compiler_params=pltpu.CompilerParams(dimension_semantics=("arbitrary",),
                                             vmem_limit_bytes=VMEM_LIMIT),
        name="post",
    )(x2d, *os_, *lses, h, h, h, gb, conv_w.astype(F32), vec(g_attn), vec(g_conv), w_o.astype(BF16),
      vec(ln1_g), vec(ln1_b), w_gate.astype(BF16), w_up.astype(BF16), w_down.astype(BF16),
      vec(ln2_g), vec(ln2_b), jnp.asarray(_head_expansion(), BF16))


def _layer(x, w_in_bf16, params):
    batch, seq, _ = x.shape
    nt = batch * seq
    x2d = x.reshape(nt, D_MODEL)
    qkv1, qkv4, qkv16, h, gb = _inproj(x2d, w_in_bf16, batch, seq, tile=1024)
    o1, l1 = _band_attention(qkv1.reshape(batch, 1, seq, 3 * ATTN_WIDTH), 1)
    o4, l4 = _band_attention(qkv4, 4)
    o16, l16 = _band_attention(qkv16, 16)
    y = _post(x2d, (o1.reshape(nt, ATTN_WIDTH), o4, o16), (l1.reshape(nt, LANES), l4, l16),
              h, gb, params, seq, tile=512)
    return y.reshape(batch, seq, D_MODEL)


def kernel(x_prompt, x_sample, w_in, conv_w, g_attn, g_conv, w_o, ln1_g, ln1_b, w_gate, w_up, w_down, ln2_g, ln2_b):
    w_in_bf16 = w_in.astype(BF16)
    params = (conv_w, g_attn, g_conv, w_o, ln1_g, ln1_b, w_gate, w_up, w_down, ln2_g, ln2_b)
    y_prompt = x_prompt
    y_sample = x_sample
    for _ in range(DEPTH):
        y_prompt = _layer(y_prompt, w_in_bf16, params)
        y_sample = _layer(y_sample, w_in_bf16, params)
    return (y_prompt, y_sample)
```
